```python
import jax, jax.numpy as jnp
from jax import lax
import numpy as np

D_MODEL = 2048
BATCH = 2
SEQ = 4096
DEPTH = 4
DEC_BATCH = 128
DEC_SEQ = 1
PAST_LEN = 8192
PAGE_SIZE = 128

POOL_WINDOWS = (2, 4, 8, 16)
N_POOL_GROUPS = 4
POOL_GROUP = 256
W_A = N_POOL_GROUPS * POOL_GROUP
POOL_HIST = max(POOL_WINDOWS) - 1
MLA_HEADS = 16
Q_LORA = 512
KV_LORA = 256
NOPE_DIM = 128
ROPE_DIM = 64
V_DIM = 128
W_B = MLA_HEADS * V_DIM
ROPE_THETA = 10000.0
MOBA_HEADS = 16
MOBA_KV_HEADS = 2
MOBA_HEAD_DIM = 64
MOBA_BLOCK = 256
MOBA_TOPK = 3
W_C = MOBA_HEADS * MOBA_HEAD_DIM
W_CKV = MOBA_KV_HEADS * MOBA_HEAD_DIM
N_BRANCH = 3
Q_BLOCK = 128
MOBA_Q_BLOCK = 64
EPS = 1e-6
NEG = -1e30
IN_SPLITS = (W_A, W_A, Q_LORA, KV_LORA, ROPE_DIM, W_B, W_C, W_CKV, W_CKV, W_C, N_BRANCH * D_MODEL)
N_IN = sum(IN_SPLITS)

kernel_name = "hybrid_pool_mla_moba_decoder_step"

F32 = jnp.float32


def rms_norm(x, g):
    xf = x.astype(F32)
    y = xf * lax.rsqrt(jnp.mean(xf * xf, axis=-1, keepdims=True) + EPS)
    return (y * g.astype(F32)).astype(x.dtype)


def rope(x, pos):
    half = ROPE_DIM // 2
    inv = ROPE_THETA ** (-jnp.arange(half, dtype=F32) / half)
    ang = pos.astype(F32)[:, None] * inv[None, :]
    shape = (1, ang.shape[0]) + (1,) * (x.ndim - 3) + (half,)
    cos = jnp.cos(ang).reshape(shape)
    sin = jnp.sin(ang).reshape(shape)
    xf = x.astype(F32)
    x1, x2 = xf[..., :half], xf[..., half:]
    return jnp.concatenate([x1 * cos - x2 * sin, x1 * sin + x2 * cos], axis=-1).astype(x.dtype)


def pool_mixer(u, hist, pos0, w_pool, pool_scale):
    n, s, _ = u.shape
    ext = jnp.concatenate([hist, u], axis=1)
    cs = jnp.cumsum(ext.astype(F32), axis=1)
    cs = jnp.concatenate([jnp.zeros((n, 1, W_A), F32), cs], axis=1)
    end = cs[:, POOL_HIST + 1:]
    pos = pos0 + jnp.arange(s)
    means = []
    for g, w in enumerate(POOL_WINDOWS):
        sl = slice(g * POOL_GROUP, (g + 1) * POOL_GROUP)
        start = cs[:, POOL_HIST + 1 - w: POOL_HIST + 1 - w + s, sl]
        cnt = jnp.minimum(w, pos + 1).astype(F32)[None, :, None]
        means.append((end[..., sl] - start) / cnt)
    mean = jnp.concatenate(means, axis=-1)
    d = (mean - u.astype(F32)).astype(u.dtype).reshape(n, s, N_POOL_GROUPS, POOL_GROUP)
    y = jnp.einsum('nsgc,gce->nsge', d, w_pool).reshape(n, s, W_A) * pool_scale
    return y, ext[:, -POOL_HIST:]


def to_chunks(a, nc, qc):
    return a.reshape((a.shape[0], nc, qc) + a.shape[2:]).swapaxes(0, 1)


def mla_attend(q_lat, q_pe, ckv, kpe, pos0):
    n, s = q_lat.shape[:2]
    t = ckv.shape[1]
    qc = min(Q_BLOCK, s)
    nc = s // qc
    scale = (NOPE_DIM + ROPE_DIM) ** -0.5
    key_pos = jnp.arange(t)

    def block(args):
        ql, qp, c = args
        logits = (jnp.einsum('nqhc,ntc->nhqt', ql, ckv, preferred_element_type=F32)
                  + jnp.einsum('nqhr,ntr->nhqt', qp, kpe, preferred_element_type=F32)) * scale
        qpos = pos0 + c * qc + jnp.arange(qc)
        logits = jnp.where(key_pos[None, None, None, :] <= qpos[None, None, :, None], logits, NEG)
        p = jax.nn.softmax(logits, axis=-1).astype(ckv.dtype)
        return jnp.einsum('nhqt,ntc->nqhc', p, ckv)

    out = lax.map(block, (to_chunks(q_lat, nc, qc), to_chunks(q_pe, nc, qc), jnp.arange(nc)))
    return out.swapaxes(0, 1).reshape(q_lat.shape)


def moba_attend(q, k_all, v_all, pos0):
    n, s = q.shape[:2]
    t = k_all.shape[1]
    nb = -(-t // MOBA_BLOCK)
    t_pad = nb * MOBA_BLOCK
    pad = ((0, 0), (0, t_pad - t), (0, 0), (0, 0))
    k_pad = jnp.pad(k_all, pad)
    v_pad = jnp.pad(v_all, pad)
    kb = k_pad.reshape(n, nb, MOBA_BLOCK, MOBA_KV_HEADS, MOBA_HEAD_DIM)
    k_mean = jnp.mean(kb.astype(F32), axis=2).astype(q.dtype)
    kb_t = kb.transpose(0, 3, 1, 2, 4)
    vb_t = v_pad.reshape(n, nb, MOBA_BLOCK, MOBA_KV_HEADS, MOBA_HEAD_DIM).transpose(0, 3, 1, 2, 4)
    topk = min(MOBA_TOPK, nb)
    grp = MOBA_HEADS // MOBA_KV_HEADS
    qc = min(MOBA_Q_BLOCK, s)
    nc = s // qc
    scale = MOBA_HEAD_DIM ** -0.5
    n_ix = jnp.arange(n)[:, None, None, None, None]
    h_ix = jnp.arange(MOBA_KV_HEADS)[None, :, None, None, None]

    def block(args):
        qb, c = args
        qg = qb.reshape(n, qc, MOBA_KV_HEADS, grp, MOBA_HEAD_DIM)
        qpos = pos0 + c * qc + jnp.arange(qc)
        cur = qpos // MOBA_BLOCK
        gate = jnp.einsum('nqkgd,nbkd->nkgqb', qg, k_mean, preferred_element_type=F32)
        gate = jnp.where(jnp.arange(nb)[None, None, None, None, :] < cur[None, None, None, :, None], gate, NEG)
        _, idx = lax.top_k(gate, topk)
        valid = jnp.arange(topk)[None, :] < cur[:, None]
        ks = kb_t[n_ix, h_ix, idx]
        vs = vb_t[n_ix, h_ix, idx]
        ls = jnp.einsum('nqkgd,nkgqjtd->nkgqjt', qg, ks, preferred_element_type=F32) * scale
        ls = jnp.where(valid[None, None, None, :, :, None], ls, NEG)
        b0 = (pos0 + c * qc) // MOBA_BLOCK
        k_own = lax.dynamic_slice_in_dim(k_pad, b0 * MOBA_BLOCK, MOBA_BLOCK, axis=1)
        v_own = lax.dynamic_slice_in_dim(v_pad, b0 * MOBA_BLOCK, MOBA_BLOCK, axis=1)
        lo = jnp.einsum('nqkgd,ntkd->nkgqt', qg, k_own, preferred_element_type=F32) * scale
        own_pos = b0 * MOBA_BLOCK + jnp.arange(MOBA_BLOCK)
        lo = jnp.where(own_pos[None, None, None, None, :] <= qpos[None, None, None, :, None], lo, NEG)
        logits = jnp.concatenate([ls.reshape(n, MOBA_KV_HEADS, grp, qc, topk * MOBA_BLOCK), lo], axis=-1)
        p = jax.nn.softmax(logits, axis=-1).astype(v_all.dtype)
        ps = p[..., :topk * MOBA_BLOCK].reshape(n, MOBA_KV_HEADS, grp, qc, topk, MOBA_BLOCK)
        po = p[..., topk * MOBA_BLOCK:]
        o = jnp.einsum('nkgqjt,nkgqjtd->nqkgd', ps, vs) + jnp.einsum('nkgqt,ntkd->nqkgd', po, v_own)
        return o.reshape(n, qc, MOBA_HEADS, MOBA_HEAD_DIM)

    out = lax.map(block, (to_chunks(q, nc, qc), jnp.arange(nc)))
    return out.swapaxes(0, 1).reshape(q.shape)


def mixer_layer(x, pos0, pool_hist, ckv_past, kpe_past, k_past, v_past,
                norm_g, w_in, w_pool, pool_scale, q_norm_g, kv_norm_g, w_uq, w_uk, w_uv,
                p_a, p_b, p_c, w_out):
    n, s, _ = x.shape
    offs = np.cumsum(IN_SPLITS)[:-1].tolist()
    h = rms_norm(x, norm_g) @ w_in
    u_a, z_a, c_q, c_kv, k_pe, z_b, q_c, k_c, v_c, z_c, g = jnp.split(h, offs, axis=-1)
    pos = pos0 + jnp.arange(s)
    y_a, pool_new = pool_mixer(u_a, pool_hist, pos0, w_pool, pool_scale)
    y_a = y_a * jax.nn.silu(z_a)
    q = (rms_norm(c_q, q_norm_g) @ w_uq).reshape(n, s, MLA_HEADS, NOPE_DIM + ROPE_DIM)
    q_pe = rope(q[..., NOPE_DIM:], pos)
    q_lat = jnp.einsum('nshd,chd->nshc', q[..., :NOPE_DIM], w_uk)
    ckv_new = rms_norm(c_kv, kv_norm_g)
    kpe_new = rope(k_pe, pos)
    o_lat = mla_attend(q_lat, q_pe, jnp.concatenate([ckv_past, ckv_new], axis=1),
                       jnp.concatenate([kpe_past, kpe_new], axis=1), pos0)
    y_b = jnp.einsum('nshc,chd->nshd', o_lat, w_uv).reshape(n, s, W_B) * jax.nn.silu(z_b)
    qh = q_c.reshape(n, s, MOBA_HEADS, MOBA_HEAD_DIM)
    kh = k_c.reshape(n, s, MOBA_KV_HEADS, MOBA_HEAD_DIM)
    vh = v_c.reshape(n, s, MOBA_KV_HEADS, MOBA_HEAD_DIM)
    y_c = moba_attend(qh, jnp.concatenate([k_past, kh], axis=1),
                      jnp.concatenate([v_past, vh], axis=1), pos0).reshape(n, s, W_C) * jax.nn.silu(z_c)
    gates = jax.nn.sigmoid(g.astype(F32)).astype(x.dtype).reshape(n, s, N_BRANCH, D_MODEL)
    m = gates[..., 0, :] * (y_a @ p_a) + gates[..., 1, :] * (y_b @ p_b) + gates[..., 2, :] * (y_c @ p_c)
    x = x + m @ w_out
    return x, (pool_new, ckv_new, kpe_new, kh, vh)


def setup_inputs(seed: int = 0) -> dict:
    key = jax.random.key(seed)
    ks = jax.random.split(key, 24)
    n_pages = PAST_LEN // PAGE_SIZE
    n_used = DEC_BATCH * n_pages
    n_pool = n_used + n_used // 4
    nrm = lambda k, shape, sc: jax.random.normal(k, shape, F32) * sc
    page_table = jax.random.permutation(ks[7], n_pool)[:n_used].reshape(DEC_BATCH, n_pages).astype(jnp.int32)
    return {
        "x_prompt": nrm(ks[0], (BATCH, SEQ, D_MODEL), 1.0),
        "x_sample": nrm(ks[1], (DEC_BATCH, DEC_SEQ, D_MODEL), 1.0),
        "state_pool": nrm(ks[2], (DEPTH, DEC_BATCH, POOL_HIST, W_A), 1.0),
        "cache_ckv": nrm(ks[3], (DEPTH, n_pool, PAGE_SIZE, KV_LORA), 1.0),
        "cache_kpe": nrm(ks[4], (DEPTH, n_pool, PAGE_SIZE, ROPE_DIM), 1.0),
        "cache_k": nrm(ks[5], (DEPTH, n_pool, PAGE_SIZE, MOBA_KV_HEADS, MOBA_HEAD_DIM), 1.0),
        "cache_v": nrm(ks[6], (DEPTH, n_pool, PAGE_SIZE, MOBA_KV_HEADS, MOBA_HEAD_DIM), 1.0),
        "page_table": page_table,
        "norm_g": 1.0 + nrm(ks[8], (DEPTH, D_MODEL), 0.02),
        "w_in": nrm(ks[9], (DEPTH, D_MODEL, N_IN), D_MODEL ** -0.5),
        "w_pool": nrm(ks[10], (DEPTH, N_POOL_GROUPS, POOL_GROUP, POOL_GROUP), POOL_GROUP ** -0.5),
        "pool_scale": 1.0 + nrm(ks[11], (DEPTH, W_A), 0.02),
        "q_norm_g": 1.0 + nrm(ks[12], (DEPTH, Q_LORA), 0.02),
        "kv_norm_g": 1.0 + nrm(ks[13], (DEPTH, KV_LORA), 0.02),
        "w_uq": nrm(ks[14], (DEPTH, Q_LORA, MLA_HEADS * (NOPE_DIM + ROPE_DIM)), Q_LORA ** -0.5),
        "w_uk": nrm(ks[15], (DEPTH, KV_LORA, MLA_HEADS, NOPE_DIM), KV_LORA ** -0.5),
        "w_uv": nrm(ks[16], (DEPTH, KV_LORA, MLA_HEADS, V_DIM), KV_LORA ** -0.5),
        "p_a": nrm(ks[17], (DEPTH, W_A, D_MODEL), W_A ** -0.5),
        "p_b": nrm(ks[18], (DEPTH, W_B, D_MODEL), W_B ** -0.5),
        "p_c": nrm(ks[19], (DEPTH, W_C, D_MODEL), W_C ** -0.5),
        "w_out": nrm(ks[20], (DEPTH, D_MODEL, D_MODEL), D_MODEL ** -0.5),
        "final_g": 1.0 + nrm(ks[21], (D_MODEL,), 0.02),
    }


def reference(x_prompt, x_sample, state_pool, cache_ckv, cache_kpe, cache_k, cache_v, page_table,
              norm_g, w_in, w_pool, pool_scale, q_norm_g, kv_norm_g, w_uq, w_uk, w_uv,
              p_a, p_b, p_c, w_out, final_g):
    def gather_pages(pool):
        gp = pool[page_table]
        return gp.reshape((gp.shape[0], gp.shape[1] * gp.shape[2]) + gp.shape[3:])

    dt = x_prompt.dtype
    empty_p = (jnp.zeros((BATCH, POOL_HIST, W_A), dt),
               jnp.zeros((BATCH, 0, KV_LORA), dt),
               jnp.zeros((BATCH, 0, ROPE_DIM), dt),
               jnp.zeros((BATCH, 0, MOBA_KV_HEADS, MOBA_HEAD_DIM), dt),
               jnp.zeros((BATCH, 0, MOBA_KV_HEADS, MOBA_HEAD_DIM), dt))
    xp, xs = x_prompt, x_sample
    new_p, new_s = [], []
    for l in range(DEPTH):
        w = (norm_g[l], w_in[l], w_pool[l], pool_scale[l], q_norm_g[l], kv_norm_g[l],
             w_uq[l], w_uk[l], w_uv[l], p_a[l], p_b[l], p_c[l], w_out[l])
        xp, st_p = mixer_layer(xp, 0, *empty_p, *w)
        xs, st_s = mixer_layer(xs, PAST_LEN, state_pool[l], gather_pages(cache_ckv[l]),
                               gather_pages(cache_kpe[l]), gather_pages(cache_k[l]),
                               gather_pages(cache_v[l]), *w)
        new_p.append(st_p)
        new_s.append(st_s)
    y_prompt = rms_norm(xp, final_g)
    y_sample = rms_norm(xs, final_g)
    pool_p = jnp.stack([st[0] for st in new_p])
    ckv_p = jnp.stack([st[1] for st in new_p])
    kpe_p = jnp.stack([st[2] for st in new_p])
    k_p = jnp.stack([st[3] for st in new_p])
    v_p = jnp.stack([st[4] for st in new_p])
    pool_s = jnp.stack([st[0] for st in new_s])
    ckv_s = jnp.stack([st[1] for st in new_s])
    kpe_s = jnp.stack([st[2] for st in new_s])
    k_s = jnp.stack([st[3] for st in new_s])
    v_s = jnp.stack([st[4] for st in new_s])
    return (y_prompt, y_sample, pool_p, ckv_p, kpe_p, k_p, v_p, pool_s, ckv_s, kpe_s, k_s, v_s)
```

```python
import functools

import jax
import jax.numpy as jnp
import numpy as np
from jax import lax
from jax.experimental import pallas as pl
from jax.experimental.pallas import tpu as pltpu

F32 = jnp.float32
BF16 = jnp.bfloat16

D_MODEL = 2048
DEPTH = 4
PAGE_SIZE = 128
POOL_WINDOWS = (2, 4, 8, 16)
POOL_GROUP = 256
W_A = len(POOL_WINDOWS) * POOL_GROUP
POOL_HIST = max(POOL_WINDOWS) - 1
MLA_HEADS = 16
Q_LORA = 512
KV_LORA = 256
NOPE_DIM = 128
ROPE_DIM = 64
V_DIM = 128
W_B = MLA_HEADS * V_DIM
ROPE_THETA = 10000.0
MOBA_HEADS = 16
MOBA_KV_HEADS = 2
MOBA_GROUP = MOBA_HEADS // MOBA_KV_HEADS
MOBA_HEAD_DIM = 64
MOBA_BLOCK = 256
MOBA_TOPK = 3
W_C = MOBA_HEADS * MOBA_HEAD_DIM
W_CKV = MOBA_KV_HEADS * MOBA_HEAD_DIM
N_BRANCH = 3
EPS = 1e-6
NEG = -1e30
IN_SPLITS = (W_A, W_A, Q_LORA, KV_LORA, ROPE_DIM, W_B, W_C, W_CKV, W_CKV, W_C, N_BRANCH * D_MODEL)

MLA_SCALE = (NOPE_DIM + ROPE_DIM) ** -0.5
MOBA_SCALE = MOBA_HEAD_DIM ** -0.5
LANES = 128
QK_WIDTH = KV_LORA + LANES

OFF_ZB = 0
OFF_UA = OFF_ZB + W_B
OFF_ZA = OFF_UA + W_A
OFF_QC = OFF_ZA + W_A
OFF_ZC = OFF_QC + W_C
OFF_G = OFF_ZC + W_C
OFF_CQ = OFF_G + N_BRANCH * D_MODEL
OFF_CKV = OFF_CQ + Q_LORA
OFF_KC = OFF_CKV + KV_LORA
OFF_VC = OFF_KC + W_CKV
OFF_KPE = OFF_VC + W_CKV
N_H = OFF_KPE + 2 * ROPE_DIM

VMEM_LIMIT = 56 * 1024 * 1024


def _params(*sem):
    return pltpu.CompilerParams(dimension_semantics=sem, vmem_limit_bytes=VMEM_LIMIT)


def _silu(z):
    return z * (1.0 / (1.0 + jnp.exp(-z)))


def _sigmoid(z):
    return 1.0 / (1.0 + jnp.exp(-z))


def _dot(a, b):
    return jnp.dot(a, b, preferred_element_type=F32)


def _dot_nt(a, b):
    return lax.dot_general(a, b, (((1,), (1,)), ((), ())), preferred_element_type=F32)


def _inproj_kernel(x_ref, g_ref, w_ref, o_ref, xn_ref):
    @pl.when(pl.program_id(1) == 0)
    def _():
        x = x_ref[...]
        ms = jnp.mean(x * x, axis=-1, keepdims=True)
        xn_ref[...] = (x * lax.rsqrt(ms + EPS) * g_ref[...]).astype(BF16)

    o_ref[...] = _dot(xn_ref[...], w_ref[...])


def _inproj(x, g, w, tm, tn):
    r, n = x.shape[0], w.shape[1]
    return pl.pallas_call(
        _inproj_kernel,
        grid=(r // tm, n // tn),
        in_specs=[pl.BlockSpec((tm, D_MODEL), lambda i, j: (i, 0)),
                  pl.BlockSpec((1, D_MODEL), lambda i, j: (0, 0)),
                  pl.BlockSpec((D_MODEL, tn), lambda i, j: (0, j))],
        out_specs=pl.BlockSpec((tm, tn), lambda i, j: (i, j)),
        out_shape=jax.ShapeDtypeStruct((r, n), F32),
        scratch_shapes=[pltpu.VMEM((tm, D_MODEL), BF16)],
        compiler_params=_params("parallel", "arbitrary"),
        name="inproj",
    )(x, g, w)


def _pool_finish(win_sum, cnt, u, z, wp_ref, sc_ref, g):
    cols = slice(g * POOL_GROUP, (g + 1) * POOL_GROUP)
    d = (win_sum / cnt - u).astype(BF16)
    y = _dot(d, wp_ref[g]) * sc_ref[:, cols]
    return (y * _silu(z)).astype(BF16)


def _pool_prompt_kernel(u_ref, z_ref, wp_ref, sc_ref, o_ref, ext_ref, *, ts):
    t = pl.program_id(1)
    halo = POOL_HIST + 1

    @pl.when(t == 0)
    def _():
        ext_ref[0:halo, :] = jnp.zeros((halo, W_A), F32)

    @pl.when(t > 0)
    def _():
        ext_ref[0:halo, :] = ext_ref[ts:ts + halo, :]

    ext_ref[halo:, :] = u_ref[...]
    pos = t * ts + lax.broadcasted_iota(jnp.int32, (ts, 1), 0)
    for g, w in enumerate(POOL_WINDOWS):
        cols = slice(g * POOL_GROUP, (g + 1) * POOL_GROUP)
        u = u_ref[:, cols]
        win_sum = u
        for k in range(1, w):
            win_sum = win_sum + ext_ref[halo - k:halo - k + ts, cols]
        cnt = jnp.minimum(w, pos + 1).astype(F32)
        o_ref[:, cols] = _pool_finish(win_sum, cnt, u, z_ref[:, cols], wp_ref, sc_ref, g)


def _pool_prompt(h, wp, sc, n_seq, seq, ts=512):
    nt = seq // ts
    blk = W_A
    return pl.pallas_call(
        functools.partial(_pool_prompt_kernel, ts=ts),
        grid=(n_seq, nt),
        in_specs=[pl.BlockSpec((ts, blk), lambda n, t: (n * nt + t, OFF_UA // blk)),
                  pl.BlockSpec((ts, blk), lambda n, t: (n * nt + t, OFF_ZA // blk)),
                  pl.BlockSpec((len(POOL_WINDOWS), POOL_GROUP, POOL_GROUP), lambda n, t: (0, 0, 0)),
                  pl.BlockSpec((1, W_A), lambda n, t: (0, 0))],
        out_specs=pl.BlockSpec((ts, blk), lambda n, t: (n * nt + t, 0)),
        out_shape=jax.ShapeDtypeStruct((n_seq * seq, W_A), BF16),
        scratch_shapes=[pltpu.VMEM((POOL_HIST + 1 + ts, W_A), F32)],
        compiler_params=_params("parallel", "arbitrary"),
        name="pool_prompt",
    )(h, h, wp, sc)


def _pool_sample_kernel(hist_ref, u_ref, z_ref, wp_ref, sc_ref, o_ref, st_ref):
    for g, w in enumerate(POOL_WINDOWS):
        cols = slice(g * POOL_GROUP, (g + 1) * POOL_GROUP)
        u = u_ref[:, cols]
        win_sum = u
        for k in range(1, w):
            off = (POOL_HIST - k) * W_A + g * POOL_GROUP
            win_sum = win_sum + hist_ref[:, off:off + POOL_GROUP]
        o_ref[:, cols] = _pool_finish(win_sum, float(w), u, z_ref[:, cols], wp_ref, sc_ref, g)
    st_ref[:, :(POOL_HIST - 1) * W_A] = hist_ref[:, W_A:]
    st_ref[:, (POOL_HIST - 1) * W_A:] = u_ref[...]


def _pool_sample(hist2d, h, wp, sc):
    b = h.shape[0]
    blk = W_A
    return pl.pallas_call(
        _pool_sample_kernel,
        grid=(1,),
        in_specs=[pl.BlockSpec((b, POOL_HIST * W_A), lambda i: (0, 0)),
                  pl.BlockSpec((b, blk), lambda i: (0, OFF_UA // blk)),
                  pl.BlockSpec((b, blk), lambda i: (0, OFF_ZA // blk)),
                  pl.BlockSpec((len(POOL_WINDOWS), POOL_GROUP, POOL_GROUP), lambda i: (0, 0, 0)),
                  pl.BlockSpec((1, W_A), lambda i: (0, 0))],
        out_specs=[pl.BlockSpec((b, W_A), lambda i: (0, 0)),
                   pl.BlockSpec((b, POOL_HIST * W_A), lambda i: (0, 0))],
        out_shape=[jax.ShapeDtypeStruct((b, W_A), BF16),
                   jax.ShapeDtypeStruct((b, POOL_HIST * W_A), F32)],
        compiler_params=_params("arbitrary"),
        name="pool_sample",
    )(hist2d, h, h, wp, sc)


def _mla_prep_kernel(cq_ref, ckv_ref, kpe_ref, cs_ref, gq_ref, gkv_ref, wqn_ref, wqp_ref, wuk_ref,
                     q_ref, kcat_ref, ckvn_ref, kper_ref):
    tm = cq_ref.shape[0]
    lane = lax.broadcasted_iota(jnp.int32, (1, LANES), 1)
    first_half = lane < ROPE_DIM
    cs = cs_ref[...]

    cq = cq_ref[...]
    ms = jnp.mean(cq * cq, axis=-1, keepdims=True)
    cqn = (cq * lax.rsqrt(ms + EPS) * gq_ref[...]).astype(BF16)
    qn = _dot(cqn, wqn_ref[...])
    qp = _dot(cqn, wqp_ref[...])
    t = qp * jnp.concatenate([cs] * MLA_HEADS, axis=1)
    rot = t + pltpu.roll(t, MLA_HEADS * LANES - ROPE_DIM, axis=1)
    for h in range(MLA_HEADS):
        hs = slice(h * LANES, (h + 1) * LANES)
        ql = _dot(qn[:, hs].astype(BF16), wuk_ref[h]) * MLA_SCALE
        q_ref[h, :, :KV_LORA] = ql.astype(BF16)
        q_ref[h, :, KV_LORA:] = jnp.where(first_half, rot[:, hs] * MLA_SCALE, 0.0).astype(BF16)

    ckv = ckv_ref[...]
    ms = jnp.mean(ckv * ckv, axis=-1, keepdims=True)
    ckvn = ckv * lax.rsqrt(ms + EPS) * gkv_ref[...]
    ckvn_ref[...] = ckvn
    kcat_ref[:, :KV_LORA] = ckvn.astype(BF16)
    tk = kpe_ref[...] * cs
    krot = tk + pltpu.roll(tk, ROPE_DIM, axis=1)
    kper_ref[...] = krot[:, :ROPE_DIM]
    kcat_ref[:, KV_LORA:] = jnp.where(first_half, krot, 0.0).astype(BF16)


def _mla_prep(h, cs, gq, gkv, wqn, wqp, wuk, tm):
    r = h.shape[0]
    const2 = lambda i: (0, 0)
    return pl.pallas_call(
        _mla_prep_kernel,
        grid=(r // tm,),
        in_specs=[pl.BlockSpec((tm, Q_LORA), lambda i: (i, OFF_CQ // Q_LORA)),
                  pl.BlockSpec((tm, KV_LORA), lambda i: (i, OFF_CKV // KV_LORA)),
                  pl.BlockSpec((tm, LANES), lambda i: (i, OFF_KPE // LANES)),
                  pl.BlockSpec((tm, LANES), lambda i: (i, 0)),
                  pl.BlockSpec((1, Q_LORA), const2),
                  pl.BlockSpec((1, KV_LORA), const2),
                  pl.BlockSpec((Q_LORA, MLA_HEADS * NOPE_DIM), const2),
                  pl.BlockSpec((Q_LORA, MLA_HEADS * LANES), const2),
                  pl.BlockSpec((MLA_HEADS, NOPE_DIM, KV_LORA), lambda i: (0, 0, 0))],
        out_specs=[pl.BlockSpec((MLA_HEADS, tm, QK_WIDTH), lambda i: (0, i, 0)),
                   pl.BlockSpec((tm, QK_WIDTH), lambda i: (i, 0)),
                   pl.BlockSpec((tm, KV_LORA), lambda i: (i, 0)),
                   pl.BlockSpec((tm, ROPE_DIM), lambda i: (i, 0))],
        out_shape=[jax.ShapeDtypeStruct((MLA_HEADS, r, QK_WIDTH), BF16),
                   jax.ShapeDtypeStruct((r, QK_WIDTH), BF16),
                   jax.ShapeDtypeStruct((r, KV_LORA), F32),
                   jax.ShapeDtypeStruct((r, ROPE_DIM), F32)],
        compiler_params=_params("parallel"),
        name="mla_prep",
    )(h, h, h, cs, gq, gkv, wqn, wqp, wuk)


def _mla_prompt_kernel(q_ref, k_ref, zb_ref, wuv_ref, o_ref, m_ref, l_ref, acc_ref, *, tq, tk):
    qi = pl.program_id(1)
    ki = pl.program_id(2)
    last = (qi * tq + tq - 1) // tk
    rows = MLA_HEADS * tq

    @pl.when(ki == 0)
    def _():
        m_ref[...] = jnp.full((rows, 1), NEG, F32)
        l_ref[...] = jnp.zeros((rows, 1), F32)
        acc_ref[...] = jnp.zeros((rows, KV_LORA), F32)

    def update(masked):
        q = q_ref[...].reshape(rows, QK_WIDTH)
        k = k_ref[...]
        s = _dot_nt(q, k)
        if masked:
            qpos = qi * tq + (lax.broadcasted_iota(jnp.int32, (rows, 1), 0) & (tq - 1))
            kpos = ki * tk + lax.broadcasted_iota(jnp.int32, (1, tk), 1)
            s = jnp.where(kpos <= qpos, s, NEG)
        m_prev = m_ref[...]
        m_new = jnp.maximum(m_prev, jnp.max(s, axis=-1, keepdims=True))
        alpha = jnp.exp(m_prev - m_new)
        p = jnp.exp(s - m_new)
        l_ref[...] = alpha * l_ref[...] + jnp.sum(p, axis=-1, keepdims=True)
        acc_ref[...] = alpha * acc_ref[...] + _dot(p.astype(BF16), k[:, :KV_LORA])
        m_ref[...] = m_new

    @pl.when(ki < last)
    def _():
        update(False)

    @pl.when(ki == last)
    def _():
        update(True)
        o = (acc_ref[...] / l_ref[...]).astype(BF16)
        for h in range(MLA_HEADS):
            hs = slice(h * V_DIM, (h + 1) * V_DIM)
            y = _dot(o[h * tq:(h + 1) * tq], wuv_ref[h])
            o_ref[:, hs] = (y * _silu(zb_ref[:, hs])).astype(BF16)


def _mla_prompt(q, kcat, h, wuv, n_seq, seq, tq=128, tk=512):
    nq, nk = seq // tq, seq // tk
    return pl.pallas_call(
        functools.partial(_mla_prompt_kernel, tq=tq, tk=tk),
        grid=(n_seq, nq, nk),
        in_specs=[pl.BlockSpec((MLA_HEADS, tq, QK_WIDTH), lambda n, i, j: (0, n * nq + i, 0)),
                  pl.BlockSpec((tk, QK_WIDTH),
                               lambda n, i, j: (n * nk + jnp.minimum(j, (i * tq + tq - 1) // tk), 0)),
                  pl.BlockSpec((tq, W_B), lambda n, i, j: (n * nq + i, OFF_ZB // W_B)),
                  pl.BlockSpec((MLA_HEADS, KV_LORA, V_DIM), lambda n, i, j: (0, 0, 0))],
        out_specs=pl.BlockSpec((tq, W_B), lambda n, i, j: (n * nq + i, 0)),
        out_shape=jax.ShapeDtypeStruct((n_seq * seq, W_B), BF16),
        scratch_shapes=[pltpu.VMEM((MLA_HEADS * tq, 1), F32),
                        pltpu.VMEM((MLA_HEADS * tq, 1), F32),
                        pltpu.VMEM((MLA_HEADS * tq, KV_LORA), F32)],
        compiler_params=_params("parallel", "parallel", "arbitrary"),
        name="mla_prompt",
    )(q, kcat, h, wuv)


def _mla_decode_kernel(pt_ref, q_ref, kn_ref, *refs, pages):
    ckv_refs = refs[:pages]
    kpe_refs = refs[pages:2 * pages]
    o_ref, m_ref, l_ref, acc_ref = refs[2 * pages:]
    st = pl.program_id(1)
    q = q_ref[...]
    ql = q[:, :KV_LORA]
    qp = q[:, KV_LORA:KV_LORA + ROPE_DIM]

    @pl.when(st == 0)
    def _():
        m_ref[...] = jnp.full((MLA_HEADS, 1), NEG, F32)
        l_ref[...] = jnp.zeros((MLA_HEADS, 1), F32)
        acc_ref[...] = jnp.zeros((MLA_HEADS, KV_LORA), F32)

    cs = [ckv_refs[i][...].astype(BF16) for i in range(pages)]
    s = jnp.concatenate(
        [_dot_nt(ql, cs[i]) + _dot_nt(qp, kpe_refs[i][...].astype(BF16)) for i in range(pages)], axis=1)
    m_prev = m_ref[...]
    m_new = jnp.maximum(m_prev, jnp.max(s, axis=-1, keepdims=True))
    alpha = jnp.exp(m_prev - m_new)
    p = jnp.exp(s - m_new).astype(BF16)
    l_ref[...] = alpha * l_ref[...] + jnp.sum(p.astype(F32), axis=-1, keepdims=True)
    pv = _dot(p[:, :PAGE_SIZE], cs[0])
    for i in range(1, pages):
        pv = pv + _dot(p[:, i * PAGE_SIZE:(i + 1) * PAGE_SIZE], cs[i])
    acc_ref[...] = alpha * acc_ref[...] + pv
    m_ref[...] = m_new

    @pl.when(st == pl.num_programs(1) - 1)
    def _():
        kn = kn_ref[...].astype(F32)
        s_new = jnp.sum(q.astype(F32) * kn, axis=-1, keepdims=True)
        m_prev = m_ref[...]
        m_fin = jnp.maximum(m_prev, s_new)
        alpha = jnp.exp(m_prev - m_fin)
        p_new = jnp.exp(s_new - m_fin)
        l_fin = alpha * l_ref[...] + p_new
        o_ref[...] = (alpha * acc_ref[...] + p_new * kn[:, :KV_LORA]) / l_fin


def _mla_decode(page_table, q, knew, cache_ckv, cache_kpe, layer, pages=8):
    b, n_pages = page_table.shape
    steps = n_pages // pages

    def ckv_spec(i):
        return pl.BlockSpec((None, None, PAGE_SIZE, KV_LORA),
                            lambda s, t, pt: (layer, pt[s, t * pages + i], 0, 0))

    def kpe_spec(i):
        return pl.BlockSpec((None, None, PAGE_SIZE, ROPE_DIM),
                            lambda s, t, pt: (layer, pt[s, t * pages + i], 0, 0))

    grid_spec = pltpu.PrefetchScalarGridSpec(
        num_scalar_prefetch=1,
        grid=(b, steps),
        in_specs=[pl.BlockSpec((None, MLA_HEADS, QK_WIDTH), lambda s, t, pt: (s, 0, 0)),
                  pl.BlockSpec((None, 1, QK_WIDTH), lambda s, t, pt: (s, 0, 0))]
                 + [ckv_spec(i) for i in range(pages)] + [kpe_spec(i) for i in range(pages)],
        out_specs=pl.BlockSpec((None, MLA_HEADS, KV_LORA), lambda s, t, pt: (s, 0, 0)),
        scratch_shapes=[pltpu.VMEM((MLA_HEADS, 1), F32),
                        pltpu.VMEM((MLA_HEADS, 1), F32),
                        pltpu.VMEM((MLA_HEADS, KV_LORA), F32)])
    return pl.pallas_call(
        functools.partial(_mla_decode_kernel, pages=pages),
        grid_spec=grid_spec,
        out_shape=jax.ShapeDtypeStruct((b, MLA_HEADS, KV_LORA), F32),
        compiler_params=_params("parallel", "arbitrary"),
        name="mla_decode",
    )(page_table, q, knew, *([cache_ckv] * pages), *([cache_kpe] * pages))


def _topk_lane_mask(gate, lane_f, n_keep):
    picked = jnp.zeros(gate.shape, F32)
    for r in range(MOBA_TOPK):
        mx = jnp.max(gate, axis=-1, keepdims=True)
        idx = jnp.min(jnp.where(gate == mx, lane_f, 1e9), axis=-1, keepdims=True)
        hit = lane_f == idx
        keep = jnp.where(r < n_keep, 1.0, 0.0)
        picked = jnp.maximum(picked, jnp.where(hit, keep, 0.0))
        gate = jnp.where(hit, -jnp.inf, gate)
    return picked


def _moba_prompt_kernel(q_ref, k_ref, v_ref, zc_ref, o_ref, km_ref, m_ref, l_ref, acc_ref):
    qb = pl.program_id(1)
    blk = MOBA_BLOCK
    rows = MOBA_GROUP * blk
    lane = lax.broadcasted_iota(jnp.int32, (1, LANES), 1)
    lane_f = lane.astype(F32)

    @pl.when(qb == 0)
    def _():
        km_ref[...] = jnp.zeros(km_ref.shape, F32)

    @pl.when(qb > 0)
    def _():
        prev = pl.multiple_of((qb - 1) * blk, blk)
        mean = jnp.sum(k_ref[pl.ds(prev, blk), :], axis=0, keepdims=True) * (1.0 / blk)
        km_ref[pl.ds(qb - 1, 1), :] = mean
        km_ref[pl.ds(MOBA_HEAD_DIM + qb - 1, 1), :] = mean

    own = pl.multiple_of(qb * blk, blk)
    k_own = k_ref[pl.ds(own, blk), :]
    v_own = v_ref[pl.ds(own, blk), :].astype(BF16)
    qf = q_ref[...]
    kmean = km_ref[...].astype(BF16)
    row_pos = lax.broadcasted_iota(jnp.int32, (rows, 1), 0) & (blk - 1)
    key_pos = lax.broadcasted_iota(jnp.int32, (1, blk), 1)
    causal = key_pos <= row_pos

    outs = []
    for c in range(MOBA_KV_HEADS):
        mine = (lane >= c * MOBA_HEAD_DIM) & (lane < (c + 1) * MOBA_HEAD_DIM)
        base = (1 - c) * MOBA_HEAD_DIM
        qg = jnp.concatenate(
            [jnp.where(mine, qf[:, g * LANES:(g + 1) * LANES], 0.0) for g in range(MOBA_GROUP)],
            axis=0) * MOBA_SCALE
        gate = _dot_nt(qg.astype(BF16), kmean)
        usable = (lane >= base) & (lane < base + qb)
        picked = _topk_lane_mask(jnp.where(usable, gate, -jnp.inf), lane_f, qb)
        bias_lanes = (lane >= base) & (lane < base + MOBA_HEAD_DIM)
        q_aug = jnp.where(bias_lanes, jnp.where(picked > 0.0, 0.0, NEG), qg).astype(BF16)

        s = _dot_nt(q_aug, jnp.where(mine, k_own, 0.0).astype(BF16))
        s = jnp.where(causal, s, NEG)
        m0 = jnp.max(s, axis=-1, keepdims=True)
        p = jnp.exp(s - m0)
        m_ref[...] = m0
        l_ref[...] = jnp.sum(p, axis=-1, keepdims=True)
        acc_ref[...] = _dot(p.astype(BF16), v_own)

        def body(j, carry):
            start = pl.multiple_of(j * blk, blk)
            k_j = jnp.where(mine, k_ref[pl.ds(start, blk), :], (lane == base + j).astype(F32))
            s = _dot_nt(q_aug, k_j.astype(BF16))
            m_prev = m_ref[...]
            m_new = jnp.maximum(m_prev, jnp.max(s, axis=-1, keepdims=True))
            alpha = jnp.exp(m_prev - m_new)
            p = jnp.exp(s - m_new)
            l_ref[...] = alpha * l_ref[...] + jnp.sum(p, axis=-1, keepdims=True)
            acc_ref[...] = alpha * acc_ref[...] + _dot(p.astype(BF16),
                                                       v_ref[pl.ds(start, blk), :].astype(BF16))
            m_ref[...] = m_new
            return carry

        lax.fori_loop(0, qb, body, 0)
        outs.append(acc_ref[...] / l_ref[...])

    first_half = lane < MOBA_HEAD_DIM
    for g in range(MOBA_GROUP):
        gs = slice(g * LANES, (g + 1) * LANES)
        o = jnp.where(first_half, outs[0][g * blk:(g + 1) * blk], outs[1][g * blk:(g + 1) * blk])
        o_ref[:, gs] = (o * _silu(zc_ref[:, gs])).astype(BF16)


def _moba_prompt(h, n_seq, seq):
    nb = seq // MOBA_BLOCK
    return pl.pallas_call(
        _moba_prompt_kernel,
        grid=(n_seq, nb),
        in_specs=[pl.BlockSpec((MOBA_BLOCK, W_C), lambda n, b: (n * nb + b, OFF_QC // W_C)),
                  pl.BlockSpec((seq, W_CKV), lambda n, b: (n, OFF_KC // W_CKV)),
                  pl.BlockSpec((seq, W_CKV), lambda n, b: (n, OFF_VC // W_CKV)),
                  pl.BlockSpec((MOBA_BLOCK, W_C), lambda n, b: (n * nb + b, OFF_ZC // W_C))],
        out_specs=pl.BlockSpec((MOBA_BLOCK, W_C), lambda n, b: (n * nb + b, 0)),
        out_shape=jax.ShapeDtypeStruct((n_seq * seq, W_C), BF16),
        scratch_shapes=[pltpu.VMEM((LANES, LANES), F32),
                        pltpu.VMEM((MOBA_GROUP * MOBA_BLOCK, 1), F32),
                        pltpu.VMEM((MOBA_GROUP * MOBA_BLOCK, 1), F32),
                        pltpu.VMEM((MOBA_GROUP * MOBA_BLOCK, LANES), F32)],
        compiler_params=_params("parallel", "arbitrary"),
        name="moba_prompt",
    )(h, h, h, h)


def _moba_decode_kernel(pt_ref, q_ref, kn_ref, vn_ref, *refs, pages, k_steps):
    k_refs = refs[:pages]
    v_refs = refs[pages:2 * pages]
    o_ref, s_ref, p_ref, km_ref, l_ref, acc_ref = refs[2 * pages:]
    st = pl.program_id(1)
    n_blocks = k_steps * pages * PAGE_SIZE // MOBA_BLOCK
    pages_per_block = MOBA_BLOCK // PAGE_SIZE
    q = q_ref[...]
    lane = lax.broadcasted_iota(jnp.int32, (1, LANES), 1)

    @pl.when(st == 0)
    def _():
        km_ref[...] = jnp.zeros(km_ref.shape, F32)

    @pl.when(st < k_steps)
    def _():
        sums = []
        for i in range(pages):
            kf = k_refs[i][...]
            s_ref[st * pages + i] = _dot_nt(q, kf.astype(BF16))
            sums.append(jnp.sum(kf, axis=0, keepdims=True))
        for b in range(pages // pages_per_block):
            tot = sums[b * pages_per_block]
            for i in range(1, pages_per_block):
                tot = tot + sums[b * pages_per_block + i]
            km_ref[pl.ds(st * (pages // pages_per_block) + b, 1), :] = tot * (1.0 / MOBA_BLOCK)

    @pl.when(st == k_steps - 1)
    def _():
        gate = _dot_nt(q, km_ref[...].astype(BF16))
        gate = jnp.where(lane < n_blocks, gate, -jnp.inf)
        picked = _topk_lane_mask(gate, lane.astype(F32), MOBA_TOPK)
        m = jnp.sum(q.astype(F32) * kn_ref[...], axis=-1, keepdims=True)
        s_new = m
        for b in range(n_blocks):
            keep = picked[:, b:b + 1] > 0.0
            for i in range(pages_per_block):
                pg = b * pages_per_block + i
                s = jnp.where(keep, s_ref[pg], NEG)
                s_ref[pg] = s
                m = jnp.maximum(m, jnp.max(s, axis=-1, keepdims=True))
        p_new = jnp.exp(s_new - m)
        l = p_new
        for pg in range(n_blocks * pages_per_block):
            p = jnp.exp(s_ref[pg] - m).astype(BF16)
            p_ref[pg] = p
            l = l + jnp.sum(p.astype(F32), axis=-1, keepdims=True)
        l_ref[...] = l
        acc_ref[...] = p_new * vn_ref[...]

    @pl.when(st >= k_steps)
    def _():
        pv = acc_ref[...]
        for i in range(pages):
            pv = pv + _dot(p_ref[(st - k_steps) * pages + i], v_refs[i][...].astype(BF16))
        acc_ref[...] = pv

    @pl.when(st == pl.num_programs(1) - 1)
    def _():
        o = acc_ref[...] / l_ref[...]
        o_ref[...] = jnp.where(lane < MOBA_HEAD_DIM, o[:MOBA_GROUP], o[MOBA_GROUP:])


def _moba_decode(page_table, q, knew, vnew, cache_k, cache_v, layer, pages=8):
    b, n_pages = page_table.shape
    k_steps = n_pages // pages

    def k_spec(i):
        return pl.BlockSpec((None, None, PAGE_SIZE, W_CKV),
                            lambda s, t, pt: (layer, pt[s, jnp.minimum(t, k_steps - 1) * pages + i], 0, 0))

    def v_spec(i):
        return pl.BlockSpec((None, None, PAGE_SIZE, W_CKV),
                            lambda s, t, pt: (layer, pt[s, jnp.maximum(t - k_steps, 0) * pages + i], 0, 0))

    grid_spec = pltpu.PrefetchScalarGridSpec(
        num_scalar_prefetch=1,
        grid=(b, 2 * k_steps),
        in_specs=[pl.BlockSpec((None, MOBA_HEADS, LANES), lambda s, t, pt: (s, 0, 0)),
                  pl.BlockSpec((None, 1, W_CKV), lambda s, t, pt: (s, 0, 0)),
                  pl.BlockSpec((None, 1, W_CKV), lambda s, t, pt: (s, 0, 0))]
                 + [k_spec(i) for i in range(pages)] + [v_spec(i) for i in range(pages)],
        out_specs=pl.BlockSpec((None, MOBA_GROUP, LANES), lambda s, t, pt: (s, 0, 0)),
        scratch_shapes=[pltpu.VMEM((n_pages, MOBA_HEADS, PAGE_SIZE), F32),
                        pltpu.VMEM((n_pages, MOBA_HEADS, PAGE_SIZE), BF16),
                        pltpu.VMEM((LANES, LANES), F32),
                        pltpu.VMEM((MOBA_HEADS, 1), F32),
                        pltpu.VMEM((MOBA_HEADS, LANES), F32)])
    return pl.pallas_call(
        functools.partial(_moba_decode_kernel, pages=pages, k_steps=k_steps),
        grid_spec=grid_spec,
        out_shape=jax.ShapeDtypeStruct((b, MOBA_GROUP, LANES), F32),
        compiler_params=_params("parallel", "arbitrary"),
        name="moba_decode",
    )(page_table, q, knew, vnew, *([cache_k] * pages), *([cache_v] * pages))


def _sample_out_kernel(ol_ref, oc_ref, zb_ref, zc_ref, wuv_ref, yb_ref, yc_ref):
    for h in range(MLA_HEADS):
        hs = slice(h * V_DIM, (h + 1) * V_DIM)
        y = _dot(ol_ref[h].astype(BF16), wuv_ref[h])
        yb_ref[:, hs] = (y * _silu(zb_ref[:, hs])).astype(BF16)
    yc_ref[...] = (oc_ref[...] * _silu(zc_ref[...])).astype(BF16)


def _sample_out(o_lat, o_c, h, wuv):
    b = h.shape[0]
    return pl.pallas_call(
        _sample_out_kernel,
        grid=(1,),
        in_specs=[pl.BlockSpec((MLA_HEADS, b, KV_LORA), lambda i: (0, 0, 0)),
                  pl.BlockSpec((b, W_C), lambda i: (0, 0)),
                  pl.BlockSpec((b, W_B), lambda i: (0, OFF_ZB // W_B)),
                  pl.BlockSpec((b, W_C), lambda i: (0, OFF_ZC // W_C)),
                  pl.BlockSpec((MLA_HEADS, KV_LORA, V_DIM), lambda i: (0, 0, 0))],
        out_specs=[pl.BlockSpec((b, W_B), lambda i: (0, 0)),
                   pl.BlockSpec((b, W_C), lambda i: (0, 0))],
        out_shape=[jax.ShapeDtypeStruct((b, W_B), BF16),
                   jax.ShapeDtypeStruct((b, W_C), BF16)],
        compiler_params=_params("arbitrary"),
        name="sample_out",
    )(o_lat, o_c, h, h, wuv)


def _merge_kernel(ya_ref, yb_ref, yc_ref, ga_ref, gb_ref, gc_ref, pa_ref, pb_ref, pc_ref, o_ref):
    m = (_sigmoid(ga_ref[...]) * _dot(ya_ref[...], pa_ref[...])
         + _sigmoid(gb_ref[...]) * _dot(yb_ref[...], pb_ref[...])
         + _sigmoid(gc_ref[...]) * _dot(yc_ref[...], pc_ref[...]))
    o_ref[...] = m.astype(BF16)


def _merge(ya, yb, yc, h, pa, pb, pc, tm, tn):
    r = ya.shape[0]
    gate_blk = lambda b: (lambda i, j: (i, (OFF_G + b * D_MODEL) // tn + j))
    row = lambda i, j: (i, 0)
    col = lambda i, j: (0, j)
    return pl.pallas_call(
        _merge_kernel,
        grid=(r // tm, D_MODEL // tn),
        in_specs=[pl.BlockSpec((tm, W_A), row), pl.BlockSpec((tm, W_B), row), pl.BlockSpec((tm, W_C), row),
                  pl.BlockSpec((tm, tn), gate_blk(0)), pl.BlockSpec((tm, tn), gate_blk(1)),
                  pl.BlockSpec((tm, tn), gate_blk(2)),
                  pl.BlockSpec((W_A, tn), col), pl.BlockSpec((W_B, tn), col), pl.BlockSpec((W_C, tn), col)],
        out_specs=pl.BlockSpec((tm, tn), lambda i, j: (i, j)),
        out_shape=jax.ShapeDtypeStruct((r, D_MODEL), BF16),
        compiler_params=_params("parallel", "arbitrary"),
        name="merge",
    )(ya, yb, yc, h, h, h, pa, pb, pc)


def _outproj_kernel(m_ref, w_ref, x_ref, o_ref):
    o_ref[...] = x_ref[...] + _dot(m_ref[...], w_ref[...])


def _outproj(m, w, x, tm, tn):
    r = m.shape[0]
    return pl.pallas_call(
        _outproj_kernel,
        grid=(r // tm, D_MODEL // tn),
        in_specs=[pl.BlockSpec((tm, D_MODEL), lambda i, j: (i, 0)),
                  pl.BlockSpec((D_MODEL, tn), lambda i, j: (0, j)),
                  pl.BlockSpec((tm, tn), lambda i, j: (i, j))],
        out_specs=pl.BlockSpec((tm, tn), lambda i, j: (i, j)),
        out_shape=jax.ShapeDtypeStruct((r, D_MODEL), F32),
        compiler_params=_params("parallel", "arbitrary"),
        name="outproj",
    )(m, w, x)


def _final_norm_kernel(x_ref, g_ref, o_ref):
    x = x_ref[...]
    ms = jnp.mean(x * x, axis=-1, keepdims=True)
    o_ref[...] = x * lax.rsqrt(ms + EPS) * g_ref[...]


def _final_norm(x, g, tm):
    r = x.shape[0]
    return pl.pallas_call(
        _final_norm_kernel,
        grid=(r // tm,),
        in_specs=[pl.BlockSpec((tm, D_MODEL), lambda i: (i, 0)),
                  pl.BlockSpec((1, D_MODEL), lambda i: (0, 0))],
        out_specs=pl.BlockSpec((tm, D_MODEL), lambda i: (i, 0)),
        out_shape=jax.ShapeDtypeStruct((r, D_MODEL), F32),
        compiler_params=_params("parallel"),
        name="final_norm",
    )(x, g)


def _pair_heads(w):
    lead = w.shape[:-1]
    w = w.reshape(lead + (MOBA_KV_HEADS, MOBA_GROUP, MOBA_HEAD_DIM))
    return jnp.swapaxes(w, -3, -2).reshape(lead + (W_C,))


def _swap_halves(w):
    half = ROPE_DIM // 2
    return jnp.concatenate([w[..., half:], w[..., :half]], axis=-1)


def _prep_weights(w_in, w_pool, w_uq, w_uk, w_uv, p_a, p_b, p_c, w_out):
    offs = np.cumsum((0,) + IN_SPLITS)
    piece = lambda i: w_in[:, :, offs[i]:offs[i + 1]]
    u_a, z_a, c_q, c_kv, k_pe, z_b, q_c, k_c, v_c, z_c, g = [piece(i) for i in range(len(IN_SPLITS))]
    w_in_p = jnp.concatenate(
        [z_b, u_a, z_a, _pair_heads(q_c), _pair_heads(z_c), g, c_q, c_kv, k_c, v_c, k_pe, _swap_halves(k_pe)],
        axis=-1).astype(BF16)
    wq = w_uq.reshape(DEPTH, Q_LORA, MLA_HEADS, NOPE_DIM + ROPE_DIM)
    wqn = wq[..., :NOPE_DIM].reshape(DEPTH, Q_LORA, MLA_HEADS * NOPE_DIM).astype(BF16)
    pe = wq[..., NOPE_DIM:]
    wqp = jnp.concatenate([pe, _swap_halves(pe)], axis=-1).reshape(DEPTH, Q_LORA, MLA_HEADS * LANES).astype(BF16)
    wuk = jnp.transpose(w_uk, (0, 2, 3, 1)).astype(BF16)
    wuv = jnp.transpose(w_uv, (0, 2, 1, 3)).astype(BF16)
    p_c_p = jnp.swapaxes(p_c.reshape(DEPTH, MOBA_KV_HEADS, MOBA_GROUP, MOBA_HEAD_DIM, D_MODEL), 1, 2)
    p_c_p = p_c_p.reshape(DEPTH, W_C, D_MODEL).astype(BF16)
    return dict(w_in=w_in_p, w_pool=w_pool.astype(BF16), wqn=wqn, wqp=wqp, wuk=wuk, wuv=wuv,
                p_a=p_a.astype(BF16), p_b=p_b.astype(BF16), p_c=p_c_p, w_out=w_out.astype(BF16))


def _rope_table(pos):
    half = ROPE_DIM // 2
    inv = ROPE_THETA ** (-jnp.arange(half, dtype=F32) / half)
    ang = pos.astype(F32)[:, None] * inv[None, :]
    cos, sin = jnp.cos(ang), jnp.sin(ang)
    return jnp.concatenate([cos, cos, -sin, sin], axis=-1)


def _layer_prompt(x, w, l, cs, norm_g, pool_scale, q_norm_g, kv_norm_g, n_seq, seq):
    h = _inproj(x, norm_g[l][None], w["w_in"][l], tm=1024, tn=896)
    ya = _pool_prompt(h, w["w_pool"][l], pool_scale[l][None], n_seq, seq)
    q, kcat, ckv_new, kpe_new = _mla_prep(h, cs, q_norm_g[l][None], kv_norm_g[l][None],
                                          w["wqn"][l], w["wqp"][l], w["wuk"][l], tm=256)
    yb = _mla_prompt(q, kcat, h, w["wuv"][l], n_seq, seq)
    yc = _moba_prompt(h, n_seq, seq)
    m = _merge(ya, yb, yc, h, w["p_a"][l], w["p_b"][l], w["p_c"][l], tm=1024, tn=512)
    x = _outproj(m, w["w_out"][l], x, tm=1024, tn=512)
    state = (h[:, OFF_UA:OFF_UA + W_A].reshape(n_seq, seq, W_A)[:, seq - POOL_HIST:],
             ckv_new.reshape(n_seq, seq, KV_LORA),
             kpe_new.reshape(n_seq, seq, ROPE_DIM),
             h[:, OFF_KC:OFF_KC + W_CKV].reshape(n_seq, seq, MOBA_KV_HEADS, MOBA_HEAD_DIM),
             h[:, OFF_VC:OFF_VC + W_CKV].reshape(n_seq, seq, MOBA_KV_HEADS, MOBA_HEAD_DIM))
    return x, state


def _layer_sample(x, w, l, cs, norm_g, pool_scale, q_norm_g, kv_norm_g,
                  hist2d, cache_ckv, cache_kpe, cache_k, cache_v, page_table):
    b = x.shape[0]
    h = _inproj(x, norm_g[l][None], w["w_in"][l], tm=b, tn=1920)
    ya, pool_new = _pool_sample(hist2d, h, w["w_pool"][l], pool_scale[l][None])
    q, kcat, ckv_new, kpe_new = _mla_prep(h, cs, q_norm_g[l][None], kv_norm_g[l][None],
                                          w["wqn"][l], w["wqp"][l], w["wuk"][l], tm=b)
    o_lat = _mla_decode(page_table, jnp.swapaxes(q, 0, 1), kcat[:, None, :], cache_ckv, cache_kpe, l)
    qc = h[:, OFF_QC:OFF_QC + W_C].reshape(b, 1, MOBA_GROUP, LANES) * MOBA_SCALE
    half =(jnp.arange(LANES)[None, None, None, :] // MOBA_HEAD_DIM
            == jnp.arange(MOBA_KV_HEADS)[None, :, None, None])
    qd = jnp.where(half, qc, 0.0).reshape(b, MOBA_HEADS, LANES).astype(BF16)
    k_new = h[:, OFF_KC:OFF_KC + W_CKV]
    v_new = h[:, OFF_VC:OFF_VC + W_CKV]
    o_c = _moba_decode(page_table, qd, k_new[:, None, :], v_new[:, None, :], cache_k, cache_v, l)
    yb, yc = _sample_out(jnp.swapaxes(o_lat, 0, 1), o_c.reshape(b, W_C), h, w["wuv"][l])
    m = _merge(ya, yb, yc, h, w["p_a"][l], w["p_b"][l], w["p_c"][l], tm=b, tn=512)
    x = _outproj(m, w["w_out"][l], x, tm=b, tn=512)
    state = (pool_new.reshape(b, POOL_HIST, W_A),
             ckv_new.reshape(b, 1, KV_LORA),
             kpe_new.reshape(b, 1, ROPE_DIM),
             k_new.reshape(b, 1, MOBA_KV_HEADS, MOBA_HEAD_DIM),
             v_new.reshape(b, 1, MOBA_KV_HEADS, MOBA_HEAD_DIM))
    return x, state


@jax.jit
def _forward(x_prompt, x_sample, state_pool, cache_ckv, cache_kpe, cache_k, cache_v, page_table,
             norm_g, w_in, w_pool, pool_scale, q_norm_g, kv_norm_g, w_uq, w_uk, w_uv,
             p_a, p_b, p_c, w_out, final_g):
    n_seq, seq, _ = x_prompt.shape
    b = x_sample.shape[0]
    past_len = page_table.shape[1] * PAGE_SIZE
    w = _prep_weights(w_in, w_pool, w_uq, w_uk, w_uv, p_a, p_b, p_c, w_out)
    cs_p = jnp.tile(_rope_table(jnp.arange(seq)), (n_seq, 1))
    cs_s = jnp.tile(_rope_table(jnp.full((1,), past_len)), (b, 1))
    cache_k2 = cache_k.reshape(cache_k.shape[:3] + (W_CKV,))
    cache_v2 = cache_v.reshape(cache_v.shape[:3] + (W_CKV,))
    hist2d = state_pool.reshape(DEPTH, b, POOL_HIST * W_A)

    xp = x_prompt.reshape(n_seq * seq, D_MODEL)
    xs = x_sample.reshape(b, D_MODEL)
    new_p, new_s = [], []
    for l in range(DEPTH):
        xp, st_p = _layer_prompt(xp, w, l, cs_p, norm_g, pool_scale, q_norm_g, kv_norm_g, n_seq, seq)
        xs, st_s = _layer_sample(xs, w, l, cs_s, norm_g, pool_scale, q_norm_g, kv_norm_g,
                                 hist2d[l], cache_ckv, cache_kpe, cache_k2, cache_v2, page_table)
        new_p.append(st_p)
        new_s.append(st_s)
    y_prompt = _final_norm(xp, final_g[None], tm=1024).reshape(n_seq, seq, D_MODEL)
    y_sample = _final_norm(xs, final_g[None], tm=b).reshape(b, 1, D_MODEL)
    stack = lambda sts, i: jnp.stack([st[i] for st in sts])
    return ((y_prompt, y_sample)
            + tuple(stack(new_p, i) for i in range(5))
            + tuple(stack(new_s, i) for i in range(5)))


def kernel(x_prompt, x_sample, state_pool, cache_ckv, cache_kpe, cache_k, cache_v, page_table,
           norm_g, w_in, w_pool, pool_scale, q_norm_g, kv_norm_g, w_uq, w_uk, w_uv,
           p_a, p_b, p_c, w_out, final_g):
    return _forward(x_prompt, x_sample, state_pool, cache_ckv, cache_kpe, cache_k, cache_v, page_table,
                    norm_g, w_in, w_pool, pool_scale, q_norm_g, kv_norm_g, w_uq, w_uk, w_uv,
                    p_a, p_b, p_c, w_out, final_g)
```

```python
import functools

import jax
import jax.numpy as jnp
import numpy as np
from jax import lax
from jax.experimental import pallas as pl
from jax.experimental.pallas import tpu as pltpu

F32 = jnp.float32
BF16 = jnp.bfloat16

D_MODEL = 2048
DEPTH = 4
PAGE_SIZE = 128
POOL_WINDOWS = (2, 4, 8, 16)
POOL_GROUP = 256
W_A = len(POOL_WINDOWS) * POOL_GROUP
POOL_HIST = max(POOL_WINDOWS) - 1
MLA_HEADS = 16
Q_LORA = 512
KV_LORA = 256
NOPE_DIM = 128
ROPE_DIM = 64
V_DIM = 128
W_B = MLA_HEADS * V_DIM
ROPE_THETA = 10000.0
MOBA_HEADS = 16
MOBA_KV_HEADS = 2
MOBA_GROUP = MOBA_HEADS // MOBA_KV_HEADS
MOBA_HEAD_DIM = 64
MOBA_BLOCK = 256
MOBA_TOPK = 3
W_C = MOBA_HEADS * MOBA_HEAD_DIM
W_CKV = MOBA_KV_HEADS * MOBA_HEAD_DIM
N_BRANCH = 3
EPS = 1e-6
NEG = -1e30
IN_SPLITS = (W_A, W_A, Q_LORA, KV_LORA, ROPE_DIM, W_B, W_C, W_CKV, W_CKV, W_C, N_BRANCH * D_MODEL)

MLA_SCALE = (NOPE_DIM + ROPE_DIM) ** -0.5
MOBA_SCALE = MOBA_HEAD_DIM ** -0.5
LANES = 128
QK_WIDTH = KV_LORA + LANES

OFF_ZB = 0
OFF_UA = OFF_ZB + W_B
OFF_ZA = OFF_UA + W_A
OFF_QC = OFF_ZA + W_A
OFF_ZC = OFF_QC + W_C
OFF_G = OFF_ZC + W_C
OFF_CQ = OFF_G + N_BRANCH * D_MODEL
OFF_CKV = OFF_CQ + Q_LORA
OFF_KC = OFF_CKV + KV_LORA
OFF_VC = OFF_KC + W_CKV
OFF_KPE = OFF_VC + W_CKV
N_H = OFF_KPE + 2 * ROPE_DIM

VMEM_LIMIT = 56 * 1024 * 1024


def _params(*sem):
    return pltpu.CompilerParams(dimension_semantics=sem, vmem_limit_bytes=VMEM_LIMIT)


def _silu(z):
    return z * (1.0 / (1.0 + jnp.exp(-z)))


def _sigmoid(z):
    return 1.0 / (1.0 + jnp.exp(-z))


def _dot(a, b):
    return jnp.dot(a, b, preferred_element_type=F32)


def _dot_nt(a, b):
    return lax.dot_general(a, b, (((1,), (1,)), ((), ())), preferred_element_type=F32)


def _inproj_kernel(x_ref, g_ref, w_ref, o_ref, xn_ref):
    @pl.when(pl.program_id(1) == 0)
    def _():
        x = x_ref[...]
        ms = jnp.mean(x * x, axis=-1, keepdims=True)
        xn_ref[...] = (x * lax.rsqrt(ms + EPS) * g_ref[...]).astype(BF16)

    o_ref[...] = _dot(xn_ref[...], w_ref[...])


def _inproj(x, g, w, tm, tn):
    r, n = x.shape[0], w.shape[1]
    return pl.pallas_call(
        _inproj_kernel,
        grid=(r // tm, n // tn),
        in_specs=[pl.BlockSpec((tm, D_MODEL), lambda i, j: (i, 0)),
                  pl.BlockSpec((1, D_MODEL), lambda i, j: (0, 0)),
                  pl.BlockSpec((D_MODEL, tn), lambda i, j: (0, j))],
        out_specs=pl.BlockSpec((tm, tn), lambda i, j: (i, j)),
        out_shape=jax.ShapeDtypeStruct((r, n), F32),
        scratch_shapes=[pltpu.VMEM((tm, D_MODEL), BF16)],
        compiler_params=_params("parallel", "arbitrary"),
        name="inproj",
    )(x, g, w)


def _pool_finish(win_sum, cnt, u, z, wp_ref, sc_ref, g):
    cols = slice(g * POOL_GROUP, (g + 1) * POOL_GROUP)
    d = (win_sum / cnt - u).astype(BF16)
    y = _dot(d, wp_ref[g]) * sc_ref[:, cols]
    return (y * _silu(z)).astype(BF16)


def _pool_prompt_kernel(u_ref, z_ref, wp_ref, sc_ref, o_ref, ext_ref, *, ts):
    t = pl.program_id(1)
    halo = POOL_HIST + 1

    @pl.when(t == 0)
    def _():
        ext_ref[0:halo, :] = jnp.zeros((halo, W_A), F32)

    @pl.when(t > 0)
    def _():
        ext_ref[0:halo, :] = ext_ref[ts:ts + halo, :]

    ext_ref[halo:, :] = u_ref[...]
    pos = t * ts + lax.broadcasted_iota(jnp.int32, (ts, 1), 0)
    for g, w in enumerate(POOL_WINDOWS):
        cols = slice(g * POOL_GROUP, (g + 1) * POOL_GROUP)
        u = u_ref[:, cols]
        win_sum = u
        for k in range(1, w):
            win_sum = win_sum + ext_ref[halo - k:halo - k + ts, cols]
        cnt = jnp.minimum(w, pos + 1).astype(F32)
        o_ref[:, cols] = _pool_finish(win_sum, cnt, u, z_ref[:, cols], wp_ref, sc_ref, g)


def _pool_prompt(h, wp, sc, n_seq, seq, ts=512):
    nt = seq // ts
    blk = W_A
    return pl.pallas_call(
        functools.partial(_pool_prompt_kernel, ts=ts),
        grid=(n_seq, nt),
        in_specs=[pl.BlockSpec((ts, blk), lambda n, t: (n * nt + t, OFF_UA // blk)),
                  pl.BlockSpec((ts, blk), lambda n, t: (n * nt + t, OFF_ZA // blk)),
                  pl.BlockSpec((len(POOL_WINDOWS), POOL_GROUP, POOL_GROUP), lambda n, t: (0, 0, 0)),
                  pl.BlockSpec((1, W_A), lambda n, t: (0, 0))],
        out_specs=pl.BlockSpec((ts, blk), lambda n, t: (n * nt + t, 0)),
        out_shape=jax.ShapeDtypeStruct((n_seq * seq, W_A), BF16),
        scratch_shapes=[pltpu.VMEM((POOL_HIST + 1 + ts, W_A), F32)],
        compiler_params=_params("parallel", "arbitrary"),
        name="pool_prompt",
    )(h, h, wp, sc)


def _pool_sample_kernel(hist_ref, u_ref, z_ref, wp_ref, sc_ref, o_ref, st_ref):
    for g, w in enumerate(POOL_WINDOWS):
        cols = slice(g * POOL_GROUP, (g + 1) * POOL_GROUP)
        u = u_ref[:, cols]
        win_sum = u
        for k in range(1, w):
            off = (POOL_HIST - k) * W_A + g * POOL_GROUP
            win_sum = win_sum + hist_ref[:, off:off + POOL_GROUP]
        o_ref[:, cols] = _pool_finish(win_sum, float(w), u, z_ref[:, cols], wp_ref, sc_ref, g)
    st_ref[:, :(POOL_HIST - 1) * W_A] = hist_ref[:, W_A:]
    st_ref[:, (POOL_HIST - 1) * W_A:] = u_ref[...]


def _pool_sample(hist2d, h, wp, sc):
    b = h.shape[0]
    blk = W_A
    return pl.pallas_call(
        _pool_sample_kernel,
        grid=(1,),
        in_specs=[pl.BlockSpec((b, POOL_HIST * W_A), lambda i: (0, 0)),
                  pl.BlockSpec((b, blk), lambda i: (0, OFF_UA // blk)),
                  pl.BlockSpec((b, blk), lambda i: (0, OFF_ZA // blk)),
                  pl.BlockSpec((len(POOL_WINDOWS), POOL_GROUP, POOL_GROUP), lambda i: (0, 0, 0)),
                  pl.BlockSpec((1, W_A), lambda i: (0, 0))],
        out_specs=[pl.BlockSpec((b, W_A), lambda i: (0, 0)),
                   pl.BlockSpec((b, POOL_HIST * W_A), lambda i: (0, 0))],
        out_shape=[jax.ShapeDtypeStruct((b, W_A), BF16),
                   jax.ShapeDtypeStruct((b, POOL_HIST * W_A), F32)],
        compiler_params=_params("arbitrary"),
        name="pool_sample",
    )(hist2d, h, h, wp, sc)


def _mla_prep_kernel(cq_ref, ckv_ref, kpe_ref, cs_ref, gq_ref, gkv_ref, wqn_ref, wqp_ref, wuk_ref,
                     q_ref, kcat_ref, ckvt_ref, ckvn_ref, kper_ref):
    lane = lax.broadcasted_iota(jnp.int32, (1, LANES), 1)
    first_half = lane < ROPE_DIM
    cs = cs_ref[...]

    cq = cq_ref[...]
    ms = jnp.mean(cq * cq, axis=-1, keepdims=True)
    cqn = (cq * lax.rsqrt(ms + EPS) * gq_ref[...]).astype(BF16)
    qn = _dot(cqn, wqn_ref[...])
    qp = _dot(cqn, wqp_ref[...])
    t = qp * jnp.concatenate([cs] * MLA_HEADS, axis=1)
    rot = t + pltpu.roll(t, MLA_HEADS * LANES - ROPE_DIM, axis=1)
    for h in range(MLA_HEADS):
        hs = slice(h * LANES, (h + 1) * LANES)
        ql = _dot(qn[:, hs].astype(BF16), wuk_ref[h]) * MLA_SCALE
        q_ref[h, :, :KV_LORA] = ql.astype(BF16)
        q_ref[h, :, KV_LORA:] = jnp.where(first_half, rot[:, hs] * MLA_SCALE, 0.0).astype(BF16)

    ckv = ckv_ref[...]
    ms = jnp.mean(ckv * ckv, axis=-1, keepdims=True)
    ckvn = ckv * lax.rsqrt(ms + EPS) * gkv_ref[...]
    ckvn_ref[...] = ckvn
    kcat_ref[:, :KV_LORA] = ckvn.astype(BF16)
    ckvt_ref[...] = ckvn.T.astype(BF16)
    tk = kpe_ref[...] * cs
    krot = tk + pltpu.roll(tk, ROPE_DIM, axis=1)
    kper_ref[...] = krot[:, :ROPE_DIM]
    kcat_ref[:, KV_LORA:] = jnp.where(first_half, krot, 0.0).astype(BF16)


def _mla_prep(h, cs, gq, gkv, wqn, wqp, wuk, tm):
    r = h.shape[0]
    const2 = lambda i: (0, 0)
    return pl.pallas_call(
        _mla_prep_kernel,
        grid=(r // tm,),
        in_specs=[pl.BlockSpec((tm, Q_LORA), lambda i: (i, OFF_CQ // Q_LORA)),
                  pl.BlockSpec((tm, KV_LORA), lambda i: (i, OFF_CKV // KV_LORA)),
                  pl.BlockSpec((tm, LANES), lambda i: (i, OFF_KPE // LANES)),
                  pl.BlockSpec((tm, LANES), lambda i: (i, 0)),
                  pl.BlockSpec((1, Q_LORA), const2),
                  pl.BlockSpec((1, KV_LORA), const2),
                  pl.BlockSpec((Q_LORA, MLA_HEADS * NOPE_DIM), const2),
                  pl.BlockSpec((Q_LORA, MLA_HEADS * LANES), const2),
                  pl.BlockSpec((MLA_HEADS, NOPE_DIM, KV_LORA), lambda i: (0, 0, 0))],
        out_specs=[pl.BlockSpec((MLA_HEADS, tm, QK_WIDTH), lambda i: (0, i, 0)),
                   pl.BlockSpec((tm, QK_WIDTH), lambda i: (i, 0)),
                   pl.BlockSpec((KV_LORA, tm), lambda i: (0, i)),
                   pl.BlockSpec((tm, KV_LORA), lambda i: (i, 0)),
                   pl.BlockSpec((tm, ROPE_DIM), lambda i: (i, 0))],
        out_shape=[jax.ShapeDtypeStruct((MLA_HEADS, r, QK_WIDTH), BF16),
                   jax.ShapeDtypeStruct((r, QK_WIDTH), BF16),
                   jax.ShapeDtypeStruct((KV_LORA, r), BF16),
                   jax.ShapeDtypeStruct((r, KV_LORA), F32),
                   jax.ShapeDtypeStruct((r, ROPE_DIM), F32)],
        compiler_params=_params("parallel"),
        name="mla_prep",
    )(h, h, h, cs, gq, gkv, wqn, wqp, wuk)


def _mla_prompt_kernel(q_ref, k_ref, kt_ref, zb_ref, wuv_ref, o_ref, m_ref, l_ref, acc_ref, *, tq, tk):
    qi = pl.program_id(1)
    ki = pl.program_id(2)
    last = (qi * tq + tq - 1) // tk
    cols = MLA_HEADS * tq

    @pl.when(ki == 0)
    def _():
        m_ref[...] = jnp.full((1, cols), NEG, F32)
        l_ref[...] = jnp.zeros((1, cols), F32)
        acc_ref[...] = jnp.zeros((KV_LORA, cols), F32)

    def update(masked):
        q = q_ref[...].reshape(cols, QK_WIDTH)
        s = _dot_nt(k_ref[...], q)
        if masked:
            qpos = qi * tq + (lax.broadcasted_iota(jnp.int32, (1, cols), 1) & (tq - 1))
            kpos = ki * tk + lax.broadcasted_iota(jnp.int32, (tk, 1), 0)
            s = jnp.where(kpos <= qpos, s, NEG)
        m_prev = m_ref[...]
        m_new = jnp.maximum(m_prev, jnp.max(s, axis=0, keepdims=True))
        alpha = jnp.exp(m_prev - m_new)
        p = jnp.exp(s - m_new)
        l_ref[...] = alpha * l_ref[...] + jnp.sum(p, axis=0, keepdims=True)
        acc_ref[...] = alpha * acc_ref[...] + _dot(kt_ref[...], p.astype(BF16))
        m_ref[...] = m_new

    @pl.when(ki < last)
    def _():
        update(False)

    @pl.when(ki == last)
    def _():
        update(True)
        o_t = acc_ref[...] / l_ref[...]
        for h in range(MLA_HEADS):
            hs = slice(h * V_DIM, (h + 1) * V_DIM)
            o_h = o_t[:, h * tq:(h + 1) * tq].T.astype(BF16)
            y = _dot(o_h, wuv_ref[h])
            o_ref[:, hs] = (y * _silu(zb_ref[:, hs])).astype(BF16)


def _mla_prompt(q, kcat, ckv_t, h, wuv, n_seq, seq, tq=128, tk=512):
    nq, nk = seq // tq, seq // tk
    k_blk = lambda n, i, j: n * nk + jnp.minimum(j, (i * tq + tq - 1) // tk)
    return pl.pallas_call(
        functools.partial(_mla_prompt_kernel, tq=tq, tk=tk),
        grid=(n_seq, nq, nk),
        in_specs=[pl.BlockSpec((MLA_HEADS, tq, QK_WIDTH), lambda n, i, j: (0, n * nq + i, 0)),
                  pl.BlockSpec((tk, QK_WIDTH), lambda n, i, j: (k_blk(n, i, j), 0)),
                  pl.BlockSpec((KV_LORA, tk), lambda n, i, j: (0, k_blk(n, i, j))),
                  pl.BlockSpec((tq, W_B), lambda n, i, j: (n * nq + i, OFF_ZB // W_B)),
                  pl.BlockSpec((MLA_HEADS, KV_LORA, V_DIM), lambda n, i, j: (0, 0, 0))],
        out_specs=pl.BlockSpec((tq, W_B), lambda n, i, j: (n * nq + i, 0)),
        out_shape=jax.ShapeDtypeStruct((n_seq * seq, W_B), BF16),
        scratch_shapes=[pltpu.VMEM((1, MLA_HEADS * tq), F32),
                        pltpu.VMEM((1, MLA_HEADS * tq), F32),
                        pltpu.VMEM((KV_LORA, MLA_HEADS * tq), F32)],
        compiler_params=_params("parallel", "parallel", "arbitrary"),
        name="mla_prompt",
    )(q, kcat, ckv_t, h, wuv)


def _mla_decode_kernel(pt_ref, q_ref, kn_ref, *refs, n_pages):
    ckv_refs = refs[:n_pages]
    kpe_refs = refs[n_pages:2 * n_pages]
    o_ref, s_ref, p_ref = refs[2 * n_pages:]
    q = q_ref[...]
    ql = q[:, :KV_LORA]
    qp = q[:, KV_LORA:KV_LORA + ROPE_DIM]
    kn = kn_ref[...].astype(F32)
    s_new = jnp.sum(q.astype(F32) * kn, axis=-1, keepdims=True)

    m_lanes = jnp.full((MLA_HEADS, PAGE_SIZE), NEG, F32)
    for i in range(n_pages):
        s = _dot_nt(ql, ckv_refs[i][...].astype(BF16)) + _dot(qp, kpe_refs[i][...].astype(BF16))
        s_ref[i] = s
        m_lanes = jnp.maximum(m_lanes, s)
    m = jnp.maximum(jnp.max(m_lanes, axis=-1, keepdims=True), s_new)
    p_new = jnp.exp(s_new - m)
    l_lanes = jnp.zeros((MLA_HEADS, PAGE_SIZE), F32)
    for i in range(n_pages):
        p = jnp.exp(s_ref[i] - m)
        l_lanes = l_lanes + p
        p_ref[i] = p.astype(BF16)
    l = p_new + jnp.sum(l_lanes, axis=-1, keepdims=True)
    acc = p_new * kn[:, :KV_LORA]
    for i in range(n_pages):
        acc = acc + _dot(p_ref[i], ckv_refs[i][...].astype(BF16))
    o_ref[...] = acc / l


def _mla_decode(page_table, q, knew, cache_ckv, cache_kpe_t, layer):
    b, n_pages = page_table.shape

    def page_spec(i, rows, cols):
        return pl.BlockSpec((None, None, rows, cols), lambda s, pt: (layer, pt[s, i], 0, 0))

    grid_spec = pltpu.PrefetchScalarGridSpec(
        num_scalar_prefetch=1,
        grid=(b,),
        in_specs=[pl.BlockSpec((None, MLA_HEADS, QK_WIDTH), lambda s, pt: (s, 0, 0)),
                  pl.BlockSpec((None, 1, QK_WIDTH), lambda s, pt: (s, 0, 0))]
                 + [page_spec(i, PAGE_SIZE, KV_LORA) for i in range(n_pages)]
                 + [page_spec(i, ROPE_DIM, PAGE_SIZE) for i in range(n_pages)],
        out_specs=pl.BlockSpec((None, MLA_HEADS, KV_LORA), lambda s, pt: (s, 0, 0)),
        scratch_shapes=[pltpu.VMEM((n_pages, MLA_HEADS, PAGE_SIZE), F32),
                        pltpu.VMEM((n_pages, MLA_HEADS, PAGE_SIZE), BF16)])
    return pl.pallas_call(
        functools.partial(_mla_decode_kernel, n_pages=n_pages),
        grid_spec=grid_spec,
        out_shape=jax.ShapeDtypeStruct((b, MLA_HEADS, KV_LORA), F32),
        compiler_params=_params("parallel"),
        name="mla_decode",
    )(page_table, q, knew, *([cache_ckv] * n_pages), *([cache_kpe_t] * n_pages))


def _topk_mask(gate, idx_f, n_keep, axis):
    picked = jnp.zeros(gate.shape, F32)
    for r in range(MOBA_TOPK):
        mx = jnp.max(gate, axis=axis, keepdims=True)
        idx = jnp.min(jnp.where(gate == mx, idx_f, 1e9), axis=axis, keepdims=True)
        hit = idx_f == idx
        keep = jnp.where(r < n_keep, 1.0, 0.0)
        picked = jnp.maximum(picked, jnp.where(hit, keep, 0.0))
        gate = jnp.where(hit, -jnp.inf, gate)
    return picked


def _moba_prompt_kernel(q_ref, k_ref, v_ref, zc_ref, o_ref,
                        kb_ref, vt_ref, km_ref, qg_ref, bias_ref, m_ref, l_ref, acc_ref):
    qb = pl.program_id(1)
    blk = MOBA_BLOCK
    cols = MOBA_GROUP * blk
    n_blocks = km_ref.shape[0]
    lane = lax.broadcasted_iota(jnp.int32, (1, LANES), 1)

    @pl.when(qb == 0)
    def _():
        km_ref[...] = jnp.zeros(km_ref.shape, F32)

    k_own = k_ref[...]
    kb = k_own.astype(BF16)
    vt = v_ref[...].T.astype(BF16)
    kb_ref[qb] = kb
    vt_ref[qb] = vt
    kmean = km_ref[...].astype(BF16)
    km_ref[pl.ds(qb, 1), :] = jnp.sum(k_own, axis=0, keepdims=True) * (1.0 / blk)

    qf = q_ref[...]
    blk_idx = lax.broadcasted_iota(jnp.int32, (n_blocks, 1), 0)
    key_idx = lax.broadcasted_iota(jnp.int32, (blk, 1), 0)
    q_idx = lax.broadcasted_iota(jnp.int32, (1, cols), 1) & (blk - 1)
    causal = key_idx <= q_idx

    outs = []
    for c in range(MOBA_KV_HEADS):
        mine = (lane >= c * MOBA_HEAD_DIM) & (lane < (c + 1) * MOBA_HEAD_DIM)
        qg_ref[...] = (jnp.concatenate(
            [jnp.where(mine, qf[:, g * LANES:(g + 1) * LANES], 0.0) for g in range(MOBA_GROUP)],
            axis=0) * MOBA_SCALE).astype(BF16)
        gate = _dot_nt(kmean, qg_ref[...])
        picked = _topk_mask(jnp.where(blk_idx < qb, gate, -jnp.inf), blk_idx.astype(F32), qb, axis=0)
        bias_ref[...] = jnp.where(picked > 0.0, 0.0, NEG)

        s = jnp.where(causal, _dot_nt(kb, qg_ref[...]), NEG)
        m0 = jnp.max(s, axis=0, keepdims=True)
        p = jnp.exp(s - m0)
        m_ref[...] = m0
        l_ref[...] = jnp.sum(p, axis=0, keepdims=True)
        acc_ref[...] = _dot(vt, p.astype(BF16))

        def body(j, carry):
            s = _dot_nt(kb_ref[j], qg_ref[...]) + bias_ref[pl.ds(j, 1), :]
            m_prev = m_ref[...]
            m_new = jnp.maximum(m_prev, jnp.max(s, axis=0, keepdims=True))
            alpha = jnp.exp(m_prev - m_new)
            p = jnp.exp(s - m_new)
            l_ref[...] = alpha * l_ref[...] + jnp.sum(p, axis=0, keepdims=True)
            acc_ref[...] = alpha * acc_ref[...] + _dot(vt_ref[j], p.astype(BF16))
            m_ref[...] = m_new
            return carry

        lax.fori_loop(0, qb, body, 0)
        outs.append(acc_ref[...] / l_ref[...])

    row = lax.broadcasted_iota(jnp.int32, (LANES, 1), 0)
    o_t = jnp.where(row < MOBA_HEAD_DIM, outs[0], outs[1])
    for g in range(MOBA_GROUP):
        gs = slice(g * LANES, (g + 1) * LANES)
        o = o_t[:, g * blk:(g + 1) * blk].T
        o_ref[:, gs] = (o * _silu(zc_ref[:, gs])).astype(BF16)


def _moba_prompt(h, n_seq, seq):
    nb = seq // MOBA_BLOCK
    cols = MOBA_GROUP * MOBA_BLOCK
    return pl.pallas_call(
        _moba_prompt_kernel,
        grid=(n_seq, nb),
        in_specs=[pl.BlockSpec((MOBA_BLOCK, W_C), lambda n, b: (n * nb + b, OFF_QC // W_C)),
                  pl.BlockSpec((MOBA_BLOCK, W_CKV), lambda n, b: (n * nb + b, OFF_KC // W_CKV)),
                  pl.BlockSpec((MOBA_BLOCK, W_CKV), lambda n, b: (n * nb + b, OFF_VC // W_CKV)),
                  pl.BlockSpec((MOBA_BLOCK, W_C), lambda n, b: (n * nb + b, OFF_ZC // W_C))],
        out_specs=pl.BlockSpec((MOBA_BLOCK, W_C), lambda n, b: (n * nb + b, 0)),
        out_shape=jax.ShapeDtypeStruct((n_seq * seq, W_C), BF16),
        scratch_shapes=[pltpu.VMEM((nb, MOBA_BLOCK, LANES), BF16),
                        pltpu.VMEM((nb, LANES, MOBA_BLOCK), BF16),
                        pltpu.VMEM((nb, LANES), F32),
                        pltpu.VMEM((cols, LANES), BF16),
                        pltpu.VMEM((nb, cols), F32),
                        pltpu.VMEM((1, cols), F32),
                        pltpu.VMEM((1, cols), F32),
                        pltpu.VMEM((LANES, cols), F32)],
        compiler_params=_params("parallel", "arbitrary"),
        name="moba_prompt",
    )(h, h, h, h)


def _moba_decode_kernel(pt_ref, q_ref, kn_ref, vn_ref, *refs, n_pages):
    k_refs = refs[:n_pages]
    v_refs = refs[n_pages:2 * n_pages]
    o_ref, s_ref, p_ref = refs[2 * n_pages:]
    pages_per_block = MOBA_BLOCK // PAGE_SIZE
    n_blocks = n_pages // pages_per_block
    q = q_ref[...]
    lane = lax.broadcasted_iota(jnp.int32, (1, LANES), 1)

    gate = jnp.full((MOBA_HEADS, LANES), -jnp.inf, F32)
    for b in range(n_blocks):
        tot = jnp.zeros((MOBA_HEADS, PAGE_SIZE), F32)
        for i in range(pages_per_block):
            pg = b * pages_per_block + i
            s = _dot(q, k_refs[pg][...].astype(BF16))
            s_ref[pg] = s
            tot = tot + s
        gate = jnp.where(lane == b, jnp.sum(tot, axis=-1, keepdims=True), gate)
    picked = _topk_mask(gate, lane.astype(F32), MOBA_TOPK, axis=1)

    s_new = jnp.sum(q.astype(F32) * kn_ref[...], axis=-1, keepdims=True)
    m_lanes = jnp.full((MOBA_HEADS, PAGE_SIZE), NEG, F32)
    for b in range(n_blocks):
        keep = picked[:, b:b + 1] > 0.0
        for i in range(pages_per_block):
            pg = b * pages_per_block + i
            s = jnp.where(keep, s_ref[pg], NEG)
            s_ref[pg] = s
            m_lanes = jnp.maximum(m_lanes, s)
    m = jnp.maximum(jnp.max(m_lanes, axis=-1, keepdims=True), s_new)
    p_new = jnp.exp(s_new - m)
    l_lanes = jnp.zeros((MOBA_HEADS, PAGE_SIZE), F32)
    for pg in range(n_pages):
        p = jnp.exp(s_ref[pg] - m)
        l_lanes = l_lanes + p
        p_ref[pg] = p.astype(BF16)
    l = p_new + jnp.sum(l_lanes, axis=-1, keepdims=True)
    acc = p_new * vn_ref[...]
    for pg in range(n_pages):
        acc = acc + _dot_nt(p_ref[pg], v_refs[pg][...].astype(BF16))
    o = acc / l
    o_ref[...] = jnp.where(lane < MOBA_HEAD_DIM, o[:MOBA_GROUP], o[MOBA_GROUP:])


def _moba_decode(page_table, q, knew, vnew, cache_k_t, cache_v_t, layer):
    b, n_pages = page_table.shape

    def page_spec(i):
        return pl.BlockSpec((None, None, W_CKV, PAGE_SIZE), lambda s, pt: (layer, pt[s, i], 0, 0))

    grid_spec = pltpu.PrefetchScalarGridSpec(
        num_scalar_prefetch=1,
        grid=(b,),
        in_specs=[pl.BlockSpec((None, MOBA_HEADS, LANES), lambda s, pt: (s, 0, 0)),
                  pl.BlockSpec((None, 1, W_CKV), lambda s, pt: (s, 0, 0)),
                  pl.BlockSpec((None, 1, W_CKV), lambda s, pt: (s, 0, 0))]
                 + [page_spec(i) for i in range(n_pages)] * 2,
        out_specs=pl.BlockSpec((None, MOBA_GROUP, LANES), lambda s, pt: (s, 0, 0)),
        scratch_shapes=[pltpu.VMEM((n_pages, MOBA_HEADS, PAGE_SIZE), F32),
                        pltpu.VMEM((n_pages, MOBA_HEADS, PAGE_SIZE), BF16)])
    return pl.pallas_call(
        functools.partial(_moba_decode_kernel, n_pages=n_pages),
        grid_spec=grid_spec,
        out_shape=jax.ShapeDtypeStruct((b, MOBA_GROUP, LANES), F32),
        compiler_params=_params("parallel"),
        name="moba_decode",
    )(page_table, q, knew, vnew, *([cache_k_t] * n_pages), *([cache_v_t] * n_pages))


def _sample_out_kernel(ol_ref, oc_ref, zb_ref, zc_ref, wuv_ref, yb_ref, yc_ref):
    for h in range(MLA_HEADS):
        hs = slice(h * V_DIM, (h + 1) * V_DIM)
        y = _dot(ol_ref[h].astype(BF16), wuv_ref[h])
        yb_ref[:, hs] = (y * _silu(zb_ref[:, hs])).astype(BF16)
    yc_ref[...] = (oc_ref[...] * _silu(zc_ref[...])).astype(BF16)


def _sample_out(o_lat, o_c, h, wuv):
    b = h.shape[0]
    return pl.pallas_call(
        _sample_out_kernel,
        grid=(1,),
        in_specs=[pl.BlockSpec((MLA_HEADS, b, KV_LORA), lambda i: (0, 0, 0)),
                  pl.BlockSpec((b, W_C), lambda i: (0, 0)),
                  pl.BlockSpec((b, W_B), lambda i: (0, OFF_ZB // W_B)),
                  pl.BlockSpec((b, W_C), lambda i: (0, OFF_ZC // W_C)),
                  pl.BlockSpec((MLA_HEADS, KV_LORA, V_DIM), lambda i: (0, 0, 0))],
        out_specs=[pl.BlockSpec((b, W_B), lambda i: (0, 0)),
                   pl.BlockSpec((b, W_C), lambda i: (0, 0))],
        out_shape=[jax.ShapeDtypeStruct((b, W_B), BF16),
                   jax.ShapeDtypeStruct((b, W_C), BF16)],
        compiler_params=_params("arbitrary"),
        name="sample_out",
    )(o_lat, o_c, h, h, wuv)


def _merge_kernel(ya_ref, yb_ref, yc_ref, ga_ref, gb_ref, gc_ref, pa_ref, pb_ref, pc_ref, o_ref):
    m = (_sigmoid(ga_ref[...]) * _dot(ya_ref[...], pa_ref[...])
         + _sigmoid(gb_ref[...]) * _dot(yb_ref[...], pb_ref[...])
         + _sigmoid(gc_ref[...]) * _dot(yc_ref[...], pc_ref[...]))
    o_ref[...] = m.astype(BF16)


def _merge(ya, yb, yc, h, pa, pb, pc, tm, tn):
    r = ya.shape[0]
    gate_blk = lambda b: (lambda i, j: (i, (OFF_G + b * D_MODEL) // tn + j))
    row = lambda i, j: (i, 0)
    col = lambda i, j: (0, j)
    return pl.pallas_call(
        _merge_kernel,
        grid=(r // tm, D_MODEL // tn),
        in_specs=[pl.BlockSpec((tm, W_A), row), pl.BlockSpec((tm, W_B), row), pl.BlockSpec((tm, W_C), row),
                  pl.BlockSpec((tm, tn), gate_blk(0)), pl.BlockSpec((tm, tn), gate_blk(1)),
                  pl.BlockSpec((tm, tn), gate_blk(2)),
                  pl.BlockSpec((W_A, tn), col), pl.BlockSpec((W_B, tn), col), pl.BlockSpec((W_C, tn), col)],
        out_specs=pl.BlockSpec((tm, tn), lambda i, j: (i, j)),
        out_shape=jax.ShapeDtypeStruct((r, D_MODEL), BF16),
        compiler_params=_params("parallel", "arbitrary"),
        name="merge",
    )(ya, yb, yc, h, h, h, pa, pb, pc)


def _outproj_kernel(m_ref, w_ref, x_ref, o_ref):
    o_ref[...] = x_ref[...] + _dot(m_ref[...], w_ref[...])


def _outproj(m, w, x, tm, tn):
    r = m.shape[0]
    return pl.pallas_call(
        _outproj_kernel,
        grid=(r // tm, D_MODEL // tn),
        in_specs=[pl.BlockSpec((tm, D_MODEL), lambda i, j: (i, 0)),
                  pl.BlockSpec((D_MODEL, tn), lambda i, j: (0, j)),
                  pl.BlockSpec((tm, tn), lambda i, j: (i, j))],
        out_specs=pl.BlockSpec((tm, tn), lambda i, j: (i, j)),
        out_shape=jax.ShapeDtypeStruct((r, D_MODEL), F32),
        compiler_params=_params("parallel", "arbitrary"),
        name="outproj",
    )(m, w, x)


def _final_norm_kernel(x_ref, g_ref, o_ref):
    x = x_ref[...]
    ms = jnp.mean(x * x, axis=-1, keepdims=True)
    o_ref[...] = x * lax.rsqrt(ms + EPS) * g_ref[...]


def _final_norm(x, g, tm):
    r = x.shape[0]
    return pl.pallas_call(
        _final_norm_kernel,
        grid=(r // tm,),
        in_specs=[pl.BlockSpec((tm, D_MODEL), lambda i: (i, 0)),
                  pl.BlockSpec((1, D_MODEL), lambda i: (0, 0))],
        out_specs=pl.BlockSpec((tm, D_MODEL), lambda i: (i, 0)),
        out_shape=jax.ShapeDtypeStruct((r, D_MODEL), F32),
        compiler_params=_params("parallel"),
        name="final_norm",
    )(x, g)


def _pair_heads(w):
    lead = w.shape[:-1]
    w = w.reshape(lead + (MOBA_KV_HEADS, MOBA_GROUP, MOBA_HEAD_DIM))
    return jnp.swapaxes(w, -3, -2).reshape(lead + (W_C,))


def _swap_halves(w):
    half = ROPE_DIM // 2
    return jnp.concatenate([w[..., half:], w[..., :half]], axis=-1)


def _prep_weights(w_in, w_pool, w_uq, w_uk, w_uv, p_a, p_b, p_c, w_out):
    offs = np.cumsum((0,) + IN_SPLITS)
    piece = lambda i: w_in[:, :, offs[i]:offs[i + 1]]
    u_a, z_a, c_q, c_kv, k_pe, z_b, q_c, k_c, v_c, z_c, g = [piece(i) for i in range(len(IN_SPLITS))]
    w_in_p = jnp.concatenate(
        [z_b, u_a, z_a, _pair_heads(q_c), _pair_heads(z_c), g, c_q, c_kv, k_c, v_c, k_pe, _swap_halves(k_pe)],
        axis=-1).astype(BF16)
    wq = w_uq.reshape(DEPTH, Q_LORA, MLA_HEADS, NOPE_DIM + ROPE_DIM)
    wqn = wq[..., :NOPE_DIM].reshape(DEPTH, Q_LORA, MLA_HEADS * NOPE_DIM).astype(BF16)
    pe = wq[..., NOPE_DIM:]
    wqp = jnp.concatenate([pe, _swap_halves(pe)], axis=-1).reshape(DEPTH, Q_LORA, MLA_HEADS * LANES).astype(BF16)
    wuk = jnp.transpose(w_uk, (0, 2, 3, 1)).astype(BF16)
    wuv = jnp.transpose(w_uv, (0, 2, 1, 3)).astype(BF16)
    p_c_p = jnp.swapaxes(p_c.reshape(DEPTH, MOBA_KV_HEADS, MOBA_GROUP, MOBA_HEAD_DIM, D_MODEL), 1, 2)
    p_c_p = p_c_p.reshape(DEPTH, W_C, D_MODEL).astype(BF16)
    return dict(w_in=w_in_p, w_pool=w_pool.astype(BF16), wqn=wqn, wqp=wqp, wuk=wuk, wuv=wuv,
                p_a=p_a.astype(BF16), p_b=p_b.astype(BF16), p_c=p_c_p, w_out=w_out.astype(BF16))


def _rope_table(pos):
    half = ROPE_DIM // 2
    inv = ROPE_THETA ** (-jnp.arange(half, dtype=F32) / half)
    ang = pos.astype(F32)[:, None] * inv[None, :]
    cos, sin = jnp.cos(ang), jnp.sin(ang)
    return jnp.concatenate([cos, cos, -sin, sin], axis=-1)


def _layer_prompt(x, w, l, cs, norm_g, pool_scale, q_norm_g, kv_norm_g, n_seq, seq):
    h = _inproj(x, norm_g[l][None], w["w_in"][l], tm=1024, tn=896)
    ya = _pool_prompt(h, w["w_pool"][l], pool_scale[l][None], n_seq, seq)
    q, kcat, ckv_t, ckv_new, kpe_new = _mla_prep(h, cs, q_norm_g[l][None], kv_norm_g[l][None],
                                                 w["wqn"][l], w["wqp"][l], w["wuk"][l], tm=256)
    yb = _mla_prompt(q, kcat, ckv_t, h, w["wuv"][l], n_seq, seq)
    yc = _moba_prompt(h, n_seq, seq)
    m = _merge(ya, yb, yc, h, w["p_a"][l], w["p_b"][l], w["p_c"][l], tm=1024, tn=512)
    x = _outproj(m, w["w_out"][l], x, tm=1024, tn=512)
    state = (h[:, OFF_UA:OFF_UA + W_A].reshape(n_seq, seq, W_A)[:, seq - POOL_HIST:],
             ckv_new.reshape(n_seq, seq, KV_LORA),
             kpe_new.reshape(n_seq, seq, ROPE_DIM),
             h[:, OFF_KC:OFF_KC + W_CKV].reshape(n_seq, seq, MOBA_KV_HEADS, MOBA_HEAD_DIM),
             h[:, OFF_VC:OFF_VC + W_CKV].reshape(n_seq, seq, MOBA_KV_HEADS, MOBA_HEAD_DIM))
    return x, state


def _layer_sample(x, w, l, cs, norm_g, pool_scale, q_norm_g, kv_norm_g,
                  hist2d, cache_ckv, cache_kpe_t, cache_k_t, cache_v_t, page_table):
    b = x.shape[0]
    h = _inproj(x, norm_g[l][None], w["w_in"][l], tm=b, tn=1920)
    ya, pool_new = _pool_sample(hist2d, h, w["w_pool"][l], pool_scale[l][None])
    q, kcat, _, ckv_new, kpe_new = _mla_prep(h, cs, q_norm_g[l][None], kv_norm_g[l][None],
                                             w["wqn"][l], w["wqp"][l], w["wuk"][l], tm=b)
    o_lat = _mla_decode(page_table, jnp.swapaxes(q, 0, 1), kcat[:, None, :], cache_ckv, cache_kpe_t, l)
    qc = h[:, OFF_QC:OFF_QC + W_C].reshape(b, 1, MOBA_GROUP, LANES) * MOBA_SCALE
    half = (jnp.arange(LANES)[None, None, None, :] // MOBA_HEAD_DIM
            == jnp.arange(MOBA_KV_HEADS)[None, :, None, None])
    qd = jnp.where(half, qc, 0.0).reshape(b, MOBA_HEADS, LANES).astype(BF16)
    k_new = h[:, OFF_KC:OFF_KC + W_CKV]
    v_new = h[:, OFF_VC:OFF_VC + W_CKV]
    o_c = _moba_decode(page_table, qd, k_new[:, None, :], v_new[:, None, :], cache_k_t, cache_v_t, l)
    yb, yc = _sample_out(jnp.swapaxes(o_lat, 0, 1), o_c.reshape(b, W_C), h, w["wuv"][l])
    m = _merge(ya, yb, yc, h, w["p_a"][l], w["p_b"][l], w["p_c"][l], tm=b, tn=512)
    x = _outproj(m, w["w_out"][l], x, tm=b, tn=512)
    state = (pool_new.reshape(b, POOL_HIST, W_A),
             ckv_new.reshape(b, 1, KV_LORA),
             kpe_new.reshape(b, 1, ROPE_DIM),
             k_new.reshape(b, 1, MOBA_KV_HEADS, MOBA_HEAD_DIM),
             v_new.reshape(b, 1, MOBA_KV_HEADS, MOBA_HEAD_DIM))
    return x, state


@jax.jit
def _forward(x_prompt, x_sample, state_pool, cache_ckv, cache_kpe, cache_k, cache_v, page_table,
             norm_g, w_in, w_pool, pool_scale, q_norm_g, kv_norm_g, w_uq, w_uk, w_uv,
             p_a, p_b, p_c, w_out, final_g):
    n_seq, seq, _ = x_prompt.shape
    b = x_sample.shape[0]
    n_pool = cache_k.shape[1]
    past_len = page_table.shape[1] * PAGE_SIZE
    w = _prep_weights(w_in, w_pool, w_uq, w_uk, w_uv, p_a, p_b, p_c, w_out)
    cs_p = jnp.tile(_rope_table(jnp.arange(seq)), (n_seq, 1))
    cs_s = jnp.tile(_rope_table(jnp.full((1,), past_len)), (b, 1))
    cache_kpe_t = jnp.swapaxes(cache_kpe, 2, 3)
    cache_k_t = jnp.transpose(cache_k, (0, 1, 3, 4, 2)).reshape(DEPTH, n_pool, W_CKV, PAGE_SIZE)
    cache_v_t = jnp.transpose(cache_v, (0, 1, 3, 4, 2)).reshape(DEPTH, n_pool, W_CKV, PAGE_SIZE)
    hist2d = state_pool.reshape(DEPTH, b, POOL_HIST * W_A)

    xp = x_prompt.reshape(n_seq * seq, D_MODEL)
    xs = x_sample.reshape(b, D_MODEL)
    new_p, new_s = [], []
    for l in range(DEPTH):
        xp, st_p = _layer_prompt(xp, w, l, cs_p, norm_g, pool_scale, q_norm_g, kv_norm_g, n_seq, seq)
        xs, st_s = _layer_sample(xs, w, l, cs_s, norm_g, pool_scale, q_norm_g, kv_norm_g,
                                 hist2d[l], cache_ckv, cache_kpe_t, cache_k_t, cache_v_t, page_table)
        new_p.append(st_p)
        new_s.append(st_s)
    y_prompt = _final_norm(xp, final_g[None], tm=1024).reshape(n_seq, seq, D_MODEL)
    y_sample = _final_norm(xs, final_g[None], tm=b).reshape(b, 1, D_MODEL)
    stack = lambda sts, i: jnp.stack([st[i] for st in sts])
    return ((y_prompt, y_sample)
            + tuple(stack(new_p, i) for i in range(5))
            + tuple(stack(new_s, i) for i in range(5)))


def kernel(x_prompt, x_sample, state_pool, cache_ckv, cache_kpe, cache_k, cache_v, page_table,
           norm_g, w_in, w_pool, pool_scale, q_norm_g, kv_norm_g, w_uq, w_uk, w_uv,
           p_a, p_b, p_c, w_out, final_g):
    return _forward(x_prompt, x_sample, state_pool, cache_ckv, cache_kpe, cache_k, cache_v, page_table,
                    norm_g, w_in, w_pool, pool_scale, q_norm_g, kv_norm_g, w_uq, w_uk, w_uv,
                    p_a, p_b, p_c, w_out, final_g)
```

```python
import functools

import jax
import jax.numpy as jnp
import numpy as np
from jax import lax
from jax.experimental import pallas as pl
from jax.experimental.pallas import tpu as pltpu

F32 = jnp.float32
BF16 = jnp.bfloat16

D_MODEL = 2048
DEPTH = 4
PAGE_SIZE = 128
POOL_WINDOWS = (2, 4, 8, 16)
POOL_GROUP = 256
W_A = len(POOL_WINDOWS) * POOL_GROUP
POOL_HIST = max(POOL_WINDOWS) - 1
MLA_HEADS = 16
Q_LORA = 512
KV_LORA = 256
NOPE_DIM = 128
ROPE_DIM = 64
V_DIM = 128
W_B = MLA_HEADS * V_DIM
ROPE_THETA = 10000.0
MOBA_HEADS = 16
MOBA_KV_HEADS = 2
MOBA_GROUP = MOBA_HEADS // MOBA_KV_HEADS
MOBA_HEAD_DIM = 64
MOBA_BLOCK = 256
MOBA_TOPK = 3
W_C = MOBA_HEADS * MOBA_HEAD_DIM
W_CKV = MOBA_KV_HEADS * MOBA_HEAD_DIM
N_BRANCH = 3
EPS = 1e-6
NEG = -1e30
IN_SPLITS = (W_A, W_A, Q_LORA, KV_LORA, ROPE_DIM, W_B, W_C, W_CKV, W_CKV, W_C, N_BRANCH * D_MODEL)

MLA_SCALE = (NOPE_DIM + ROPE_DIM) ** -0.5
MOBA_SCALE = MOBA_HEAD_DIM ** -0.5
LANES = 128
QK_WIDTH = KV_LORA + LANES

OFF_ZB = 0
OFF_UA = OFF_ZB + W_B
OFF_ZA = OFF_UA + W_A
OFF_QC = OFF_ZA + W_A
OFF_ZC = OFF_QC + W_C
OFF_G = OFF_ZC + W_C
OFF_CQ = OFF_G + N_BRANCH * D_MODEL
OFF_CKV = OFF_CQ + Q_LORA
OFF_KC = OFF_CKV + KV_LORA
OFF_VC = OFF_KC + W_CKV
OFF_KPE = OFF_VC + W_CKV
N_H = OFF_KPE + 2 * ROPE_DIM

VMEM_LIMIT = 56 * 1024 * 1024


def _params(*sem):
    return pltpu.CompilerParams(dimension_semantics=sem, vmem_limit_bytes=VMEM_LIMIT)


def _silu(z):
    return z * (1.0 / (1.0 + jnp.exp(-z)))


def _sigmoid(z):
    return 1.0 / (1.0 + jnp.exp(-z))


def _dot(a, b):
    return jnp.dot(a, b, preferred_element_type=F32)


def _dot_nt(a, b):
    return lax.dot_general(a, b, (((1,), (1,)), ((), ())), preferred_element_type=F32)


def _inproj_kernel(x_ref, g_ref, w_ref, o_ref, xn_ref):
    @pl.when(pl.program_id(1) == 0)
    def _():
        x = x_ref[...]
        ms = jnp.mean(x * x, axis=-1, keepdims=True)
        xn_ref[...] = (x * lax.rsqrt(ms + EPS) * g_ref[...]).astype(BF16)

    o_ref[...] = _dot_nt(xn_ref[...], w_ref[...])


def _inproj(x, g, w, tm, tn):
    r, n = x.shape[0], w.shape[0]
    return pl.pallas_call(
        _inproj_kernel,
        grid=(r // tm, n // tn),
        in_specs=[pl.BlockSpec((tm, D_MODEL), lambda i, j: (i, 0)),
                  pl.BlockSpec((1, D_MODEL), lambda i, j: (0, 0)),
                  pl.BlockSpec((tn, D_MODEL), lambda i, j: (j, 0))],
        out_specs=pl.BlockSpec((tm, tn), lambda i, j: (i, j)),
        out_shape=jax.ShapeDtypeStruct((r, n), F32),
        scratch_shapes=[pltpu.VMEM((tm, D_MODEL), BF16)],
        compiler_params=_params("parallel", "arbitrary"),
        name="inproj",
    )(x, g, w)


def _pool_finish(win_sum, cnt, u, z, wp_ref, sc_ref, g):
    cols = slice(g * POOL_GROUP, (g + 1) * POOL_GROUP)
    d = (win_sum / cnt - u).astype(BF16)
    y = _dot(d, wp_ref[g]) * sc_ref[:, cols]
    return (y * _silu(z)).astype(BF16)


def _pool_prompt_kernel(u_ref, z_ref, wp_ref, sc_ref, o_ref, ext_ref, *, ts):
    t = pl.program_id(1)
    halo = POOL_HIST + 1

    @pl.when(t == 0)
    def _():
        ext_ref[0:halo, :] = jnp.zeros((halo, W_A), F32)

    @pl.when(t > 0)
    def _():
        ext_ref[0:halo, :] = ext_ref[ts:ts + halo, :]

    ext_ref[halo:, :] = u_ref[...]
    pos = t * ts + lax.broadcasted_iota(jnp.int32, (ts, 1), 0)
    for g, w in enumerate(POOL_WINDOWS):
        cols = slice(g * POOL_GROUP, (g + 1) * POOL_GROUP)
        u = u_ref[:, cols]
        win_sum = u
        for k in range(1, w):
            win_sum = win_sum + ext_ref[halo - k:halo - k + ts, cols]
        cnt = jnp.minimum(w, pos + 1).astype(F32)
        o_ref[:, cols] = _pool_finish(win_sum, cnt, u, z_ref[:, cols], wp_ref, sc_ref, g)


def _pool_prompt(h, wp, sc, n_seq, seq, ts=512):
    nt = seq // ts
    blk = W_A
    return pl.pallas_call(
        functools.partial(_pool_prompt_kernel, ts=ts),
        grid=(n_seq, nt),
        in_specs=[pl.BlockSpec((ts, blk), lambda n, t: (n * nt + t, OFF_UA // blk)),
                  pl.BlockSpec((ts, blk), lambda n, t: (n * nt + t, OFF_ZA // blk)),
                  pl.BlockSpec((len(POOL_WINDOWS), POOL_GROUP, POOL_GROUP), lambda n, t: (0, 0, 0)),
                  pl.BlockSpec((1, W_A), lambda n, t: (0, 0))],
        out_specs=pl.BlockSpec((ts, blk), lambda n, t: (n * nt + t, 0)),
        out_shape=jax.ShapeDtypeStruct((n_seq * seq, W_A), BF16),
        scratch_shapes=[pltpu.VMEM((POOL_HIST + 1 + ts, W_A), F32)],
        compiler_params=_params("parallel", "arbitrary"),
        name="pool_prompt",
    )(h, h, wp, sc)


def _pool_sample_kernel(hist_ref, u_ref, z_ref, wp_ref, sc_ref, o_ref, st_ref):
    for g, w in enumerate(POOL_WINDOWS):
        cols = slice(g * POOL_GROUP, (g + 1) * POOL_GROUP)
        u = u_ref[:, cols]
        win_sum = u
        for k in range(1, w):
            off = (POOL_HIST - k) * W_A + g * POOL_GROUP
            win_sum = win_sum + hist_ref[:, off:off + POOL_GROUP]
        o_ref[:, cols] = _pool_finish(win_sum, float(w), u, z_ref[:, cols], wp_ref, sc_ref, g)
    st_ref[:, :(POOL_HIST - 1) * W_A] = hist_ref[:, W_A:]
    st_ref[:, (POOL_HIST - 1) * W_A:] = u_ref[...]


def _pool_sample(hist2d, h, wp, sc):
    b = h.shape[0]
    blk = W_A
    return pl.pallas_call(
        _pool_sample_kernel,
        grid=(1,),
        in_specs=[pl.BlockSpec((b, POOL_HIST * W_A), lambda i: (0, 0)),
                  pl.BlockSpec((b, blk), lambda i: (0, OFF_UA // blk)),
                  pl.BlockSpec((b, blk), lambda i: (0, OFF_ZA // blk)),
                  pl.BlockSpec((len(POOL_WINDOWS), POOL_GROUP, POOL_GROUP), lambda i: (0, 0, 0)),
                  pl.BlockSpec((1, W_A), lambda i: (0, 0))],
        out_specs=[pl.BlockSpec((b, W_A), lambda i: (0, 0)),
                   pl.BlockSpec((b, POOL_HIST * W_A), lambda i: (0, 0))],
        out_shape=[jax.ShapeDtypeStruct((b, W_A), BF16),
                   jax.ShapeDtypeStruct((b, POOL_HIST * W_A), F32)],
        compiler_params=_params("arbitrary"),
        name="pool_sample",
    )(hist2d, h, h, wp, sc)


def _mla_prep_kernel(cq_ref, ckv_ref, kpe_ref, cs_ref, gq_ref, gkv_ref, wqn_ref, wqp_ref, wuk_ref,
                     q_ref, kcat_ref, ckvt_ref, ckvn_ref, kper_ref):
    lane = lax.broadcasted_iota(jnp.int32, (1, LANES), 1)
    first_half = lane < ROPE_DIM
    cs = cs_ref[...]

    cq = cq_ref[...]
    ms = jnp.mean(cq * cq, axis=-1, keepdims=True)
    cqn = (cq * lax.rsqrt(ms + EPS) * gq_ref[...]).astype(BF16)
    qn = _dot(cqn, wqn_ref[...])
    qp = _dot(cqn, wqp_ref[...])
    t = qp * jnp.concatenate([cs] * MLA_HEADS, axis=1)
    rot = t + pltpu.roll(t, MLA_HEADS * LANES - ROPE_DIM, axis=1)
    for h in range(MLA_HEADS):
        hs = slice(h * LANES, (h + 1) * LANES)
        ql = _dot(qn[:, hs].astype(BF16), wuk_ref[h]) * MLA_SCALE
        q_ref[h, :, :KV_LORA] = ql.astype(BF16)
        q_ref[h, :, KV_LORA:] = jnp.where(first_half, rot[:, hs] * MLA_SCALE, 0.0).astype(BF16)

    ckv = ckv_ref[...]
    ms = jnp.mean(ckv * ckv, axis=-1, keepdims=True)
    ckvn = ckv * lax.rsqrt(ms + EPS) * gkv_ref[...]
    ckvn_ref[...] = ckvn
    kcat_ref[:, :KV_LORA] = ckvn.astype(BF16)
    ckvt_ref[...] = ckvn.T.astype(BF16)
    tk = kpe_ref[...] * cs
    krot = tk + pltpu.roll(tk, ROPE_DIM, axis=1)
    kper_ref[...] = krot[:, :ROPE_DIM]
    kcat_ref[:, KV_LORA:] = jnp.where(first_half, krot, 0.0).astype(BF16)


def _mla_prep(h, cs, gq, gkv, wqn, wqp, wuk, tm):
    r = h.shape[0]
    const2 = lambda i: (0, 0)
    return pl.pallas_call(
        _mla_prep_kernel,
        grid=(r // tm,),
        in_specs=[pl.BlockSpec((tm, Q_LORA), lambda i: (i, OFF_CQ // Q_LORA)),
                  pl.BlockSpec((tm, KV_LORA), lambda i: (i, OFF_CKV // KV_LORA)),
                  pl.BlockSpec((tm, LANES), lambda i: (i, OFF_KPE // LANES)),
                  pl.BlockSpec((tm, LANES), lambda i: (i, 0)),
                  pl.BlockSpec((1, Q_LORA), const2),
                  pl.BlockSpec((1, KV_LORA), const2),
                  pl.BlockSpec((Q_LORA, MLA_HEADS * NOPE_DIM), const2),
                  pl.BlockSpec((Q_LORA, MLA_HEADS * LANES), const2),
                  pl.BlockSpec((MLA_HEADS, NOPE_DIM, KV_LORA), lambda i: (0, 0, 0))],
        out_specs=[pl.BlockSpec((MLA_HEADS, tm, QK_WIDTH), lambda i: (0, i, 0)),
                   pl.BlockSpec((tm, QK_WIDTH), lambda i: (i, 0)),
                   pl.BlockSpec((KV_LORA, tm), lambda i: (0, i)),
                   pl.BlockSpec((tm, KV_LORA), lambda i: (i, 0)),
                   pl.BlockSpec((tm, ROPE_DIM), lambda i: (i, 0))],
        out_shape=[jax.ShapeDtypeStruct((MLA_HEADS, r, QK_WIDTH), BF16),
                   jax.ShapeDtypeStruct((r, QK_WIDTH), BF16),
                   jax.ShapeDtypeStruct((KV_LORA, r), BF16),
                   jax.ShapeDtypeStruct((r, KV_LORA), F32),
                   jax.ShapeDtypeStruct((r, ROPE_DIM), F32)],
        compiler_params=_params("parallel"),
        name="mla_prep",
    )(h, h, h, cs, gq, gkv, wqn, wqp, wuk)


def _mla_prompt_kernel(qi_ref, ki_ref, q_ref, k_ref, kt_ref, zb_ref, wuv_ref, o_ref, m_ref, l_ref, acc_ref,
                       *, tq, tk):
    qi = qi_ref[pl.program_id(1)]
    ki = ki_ref[pl.program_id(1)]
    last = (qi * tq + tq - 1) // tk
    cols = MLA_HEADS * tq

    @pl.when(ki == 0)
    def _():
        m_ref[...] = jnp.full((1, cols), NEG, F32)
        l_ref[...] = jnp.zeros((1, cols), F32)
        acc_ref[...] = jnp.zeros((KV_LORA, cols), F32)

    def update(masked):
        q = q_ref[...].reshape(cols, QK_WIDTH)
        s = _dot_nt(k_ref[...], q)
        if masked:
            qpos = qi * tq + (lax.broadcasted_iota(jnp.int32, (1, cols), 1) & (tq - 1))
            kpos = ki * tk + lax.broadcasted_iota(jnp.int32, (tk, 1), 0)
            s = jnp.where(kpos <= qpos, s, NEG)
        m_prev = m_ref[...]
        m_new = jnp.maximum(m_prev, jnp.max(s, axis=0, keepdims=True))
        alpha = jnp.exp(m_prev - m_new)
        p = jnp.exp(s - m_new)
        l_ref[...] = alpha * l_ref[...] + jnp.sum(p, axis=0, keepdims=True)
        acc_ref[...] = alpha * acc_ref[...] + _dot(kt_ref[...], p.astype(BF16))
        m_ref[...] = m_new

    @pl.when(ki < last)
    def _():
        update(False)

    @pl.when(ki == last)
    def _():
        update(True)
        o_t = acc_ref[...] / l_ref[...]
        for h in range(MLA_HEADS):
            hs = slice(h * V_DIM, (h + 1) * V_DIM)
            o_h = o_t[:, h * tq:(h + 1) * tq].T.astype(BF16)
            y = _dot(o_h, wuv_ref[h])
            o_ref[:, hs] = (y * _silu(zb_ref[:, hs])).astype(BF16)


def _mla_prompt(q, kcat, ckv_t, h, wuv, n_seq, seq, tq=128, tk=512):
    nq, nk = seq // tq, seq // tk
    pairs = [(i, j) for i in range(nq) for j in range((i * tq + tq - 1) // tk + 1)]
    qi_tab = jnp.asarray([p[0] for p in pairs], jnp.int32)
    ki_tab = jnp.asarray([p[1] for p in pairs], jnp.int32)
    grid_spec = pltpu.PrefetchScalarGridSpec(
        num_scalar_prefetch=2,
        grid=(n_seq, len(pairs)),
        in_specs=[pl.BlockSpec((MLA_HEADS, tq, QK_WIDTH), lambda n, t, qi, ki: (0, n * nq + qi[t], 0)),
                  pl.BlockSpec((tk, QK_WIDTH), lambda n, t, qi, ki: (n * nk + ki[t], 0)),
                  pl.BlockSpec((KV_LORA, tk), lambda n, t, qi, ki: (0, n * nk + ki[t])),
                  pl.BlockSpec((tq, W_B), lambda n, t, qi, ki: (n * nq + qi[t], OFF_ZB // W_B)),
                  pl.BlockSpec((MLA_HEADS, KV_LORA, V_DIM), lambda n, t, qi, ki: (0, 0, 0))],
        out_specs=pl.BlockSpec((tq, W_B), lambda n, t, qi, ki: (n * nq + qi[t], 0)),
        scratch_shapes=[pltpu.VMEM((1, MLA_HEADS * tq), F32),
                        pltpu.VMEM((1, MLA_HEADS * tq), F32),
                        pltpu.VMEM((KV_LORA, MLA_HEADS * tq), F32)])
    return pl.pallas_call(
        functools.partial(_mla_prompt_kernel, tq=tq, tk=tk),
        grid_spec=grid_spec,
        out_shape=jax.ShapeDtypeStruct((n_seq * seq, W_B), BF16),
        compiler_params=_params("parallel", "arbitrary"),
        name="mla_prompt",
    )(qi_tab, ki_tab, q, kcat, ckv_t, h, wuv)


def _page_copies(pt_ref, seq, slot, layer, n_pages, caches, bufs, sems):
    copies = []
    for i in range(n_pages):
        page = pt_ref[seq, i]
        for a, (cache, buf) in enumerate(zip(caches, bufs)):
            copies.append(pltpu.make_async_copy(cache.at[layer, page], buf.at[slot, i], sems.at[a, slot]))
    return copies


def _paged_prefetch(pt_ref, layer, n_pages, caches, bufs, sems):
    seq = pl.program_id(0)
    slot = seq % 2

    @pl.when(seq == 0)
    def _():
        for c in _page_copies(pt_ref, seq, slot, layer, n_pages, caches, bufs, sems):
            c.start()

    @pl.when(seq + 1 < pl.num_programs(0))
    def _():
        for c in _page_copies(pt_ref, seq + 1, 1 - slot, layer, n_pages, caches, bufs, sems):
            c.start()

    for c in _page_copies(pt_ref, seq, slot, layer, n_pages, caches, bufs, sems):
        c.wait()
    return slot


def _mla_decode_kernel(pt_ref, q_ref, kn_ref, ckv_hbm, kpe_hbm, o_ref, ckv_buf, kpe_buf, sems, s_ref, p_ref,
                       *, layer, n_pages):
    slot = _paged_prefetch(pt_ref, layer, n_pages, (ckv_hbm, kpe_hbm), (ckv_buf, kpe_buf), sems)
    ckv_refs = [ckv_buf.at[slot, i] for i in range(n_pages)]
    kpe_refs = [kpe_buf.at[slot, i] for i in range(n_pages)]
    q = q_ref[...]
    ql = q[:, :KV_LORA]
    qp = q[:, KV_LORA:KV_LORA + ROPE_DIM]
    kn = kn_ref[...].astype(F32)
    s_new = jnp.sum(q.astype(F32) * kn, axis=-1, keepdims=True)

    m_lanes = jnp.full((MLA_HEADS, PAGE_SIZE), NEG, F32)
    for i in range(n_pages):
        s = _dot_nt(ql, ckv_refs[i][...].astype(BF16)) + _dot(qp, kpe_refs[i][...].astype(BF16))
        s_ref[i] = s
        m_lanes = jnp.maximum(m_lanes, s)
    m = jnp.maximum(jnp.max(m_lanes, axis=-1, keepdims=True), s_new)
    p_new = jnp.exp(s_new - m)
    l_lanes = jnp.zeros((MLA_HEADS, PAGE_SIZE), F32)
    for i in range(n_pages):
        p = jnp.exp(s_ref[i] - m)
        l_lanes = l_lanes + p
        p_ref[i] = p.astype(BF16)
    l = p_new + jnp.sum(l_lanes, axis=-1, keepdims=True)
    acc = p_new * kn[:, :KV_LORA]
    for i in range(n_pages):
        acc = acc + _dot(p_ref[i], ckv_refs[i][...].astype(BF16))
    o_ref[...] = acc / l


def _mla_decode(page_table, q, knew, cache_ckv, cache_kpe_t, layer):
    b, n_pages = page_table.shape
    grid_spec = pltpu.PrefetchScalarGridSpec(
        num_scalar_prefetch=1,
        grid=(b,),
        in_specs=[pl.BlockSpec((None, MLA_HEADS, QK_WIDTH), lambda s, pt: (s, 0, 0)),
                  pl.BlockSpec((None, 1, QK_WIDTH), lambda s, pt: (s, 0, 0)),
                  pl.BlockSpec(memory_space=pl.ANY),
                  pl.BlockSpec(memory_space=pl.ANY)],
        out_specs=pl.BlockSpec((None, MLA_HEADS, KV_LORA), lambda s, pt: (s, 0, 0)),
        scratch_shapes=[pltpu.VMEM((2, n_pages, PAGE_SIZE, KV_LORA), F32),
                        pltpu.VMEM((2, n_pages, ROPE_DIM, PAGE_SIZE), F32),
                        pltpu.SemaphoreType.DMA((2, 2)),
                        pltpu.VMEM((n_pages, MLA_HEADS, PAGE_SIZE), F32),
                        pltpu.VMEM((n_pages, MLA_HEADS, PAGE_SIZE), BF16)])
    return pl.pallas_call(
        functools.partial(_mla_decode_kernel, layer=layer, n_pages=n_pages),
        grid_spec=grid_spec,
        out_shape=jax.ShapeDtypeStruct((b, MLA_HEADS, KV_LORA), F32),
        compiler_params=_params("arbitrary"),
        name="mla_decode",
    )(page_table, q, knew, cache_ckv, cache_kpe_t)


def _topk_mask(gate, idx_f, n_keep, axis):
    picked = jnp.zeros(gate.shape, F32)
    for r in range(MOBA_TOPK):
        mx = jnp.max(gate, axis=axis, keepdims=True)
        idx = jnp.min(jnp.where(gate == mx, idx_f, 1e9), axis=axis, keepdims=True)
        hit = idx_f == idx
        keep = jnp.where(r < n_keep, 1.0, 0.0)
        picked = jnp.maximum(picked, jnp.where(hit, keep, 0.0))
        gate = jnp.where(hit, -jnp.inf, gate)
    return picked


def _moba_prompt_kernel(q_ref, k_ref, v_ref, zc_ref, o_ref,
                        kb_ref, vt_ref, km_ref, qg_ref, bias_ref, m_ref, l_ref, acc_ref):
    qb = pl.program_id(1)
    blk = MOBA_BLOCK
    cols = MOBA_GROUP * blk
    n_blocks = km_ref.shape[0]
    lane = lax.broadcasted_iota(jnp.int32, (1, LANES), 1)

    @pl.when(qb == 0)
    def _():
        km_ref[...] = jnp.zeros(km_ref.shape, F32)

    k_own = k_ref[...]
    kb = k_own.astype(BF16)
    vt = v_ref[...].T.astype(BF16)
    kb_ref[qb] = kb
    vt_ref[qb] = vt
    kmean = km_ref[...].astype(BF16)
    km_ref[pl.ds(qb, 1), :] = jnp.sum(k_own, axis=0, keepdims=True) * (1.0 / blk)

    qf = q_ref[...]
    blk_idx = lax.broadcasted_iota(jnp.int32, (n_blocks, 1), 0)
    key_idx = lax.broadcasted_iota(jnp.int32, (blk, 1), 0)
    q_idx = lax.broadcasted_iota(jnp.int32, (1, cols), 1) & (blk - 1)
    causal = key_idx <= q_idx

    outs = []
    for c in range(MOBA_KV_HEADS):
        mine = (lane >= c * MOBA_HEAD_DIM) & (lane < (c + 1) * MOBA_HEAD_DIM)
        qg_ref[...] = (jnp.concatenate(
            [jnp.where(mine, qf[:, g * LANES:(g + 1) * LANES], 0.0) for g in range(MOBA_GROUP)],
            axis=0) * MOBA_SCALE).astype(BF16)
        gate = _dot_nt(kmean, qg_ref[...])
        picked = _topk_mask(jnp.where(blk_idx < qb, gate, -jnp.inf), blk_idx.astype(F32), qb, axis=0)
        bias_ref[...] = jnp.where(picked > 0.0, 0.0, NEG)

        s = jnp.where(causal, _dot_nt(kb, qg_ref[...]), NEG)
        m0 = jnp.max(s, axis=0, keepdims=True)
        p = jnp.exp(s - m0)
        m_ref[...] = m0
        l_ref[...] = jnp.sum(p, axis=0, keepdims=True)
        acc_ref[...] = _dot(vt, p.astype(BF16))

        def body(j, carry):
            s = _dot_nt(kb_ref[j], qg_ref[...]) + bias_ref[pl.ds(j, 1), :]
            m_prev = m_ref[...]
            m_new = jnp.maximum(m_prev, jnp.max(s, axis=0, keepdims=True))
            alpha = jnp.exp(m_prev - m_new)
            p = jnp.exp(s - m_new)
            l_ref[...] = alpha * l_ref[...] + jnp.sum(p, axis=0, keepdims=True)
            acc_ref[...] = alpha * acc_ref[...] + _dot(vt_ref[j], p.astype(BF16))
            m_ref[...] = m_new
            return carry

        lax.fori_loop(0, qb, body, 0)
        outs.append(acc_ref[...] / l_ref[...])

    row = lax.broadcasted_iota(jnp.int32, (LANES, 1), 0)
    o_t = jnp.where(row < MOBA_HEAD_DIM, outs[0], outs[1])
    for g in range(MOBA_GROUP):
        gs = slice(g * LANES, (g + 1) * LANES)
        o = o_t[:, g * blk:(g + 1) * blk].T
        o_ref[:, gs] = (o * _silu(zc_ref[:, gs])).astype(BF16)


def _moba_prompt(h, n_seq, seq):
    nb = seq // MOBA_BLOCK
    cols = MOBA_GROUP * MOBA_BLOCK
    return pl.pallas_call(
        _moba_prompt_kernel,
        grid=(n_seq, nb),
        in_specs=[pl.BlockSpec((MOBA_BLOCK, W_C), lambda n, b: (n * nb + b, OFF_QC // W_C)),
                  pl.BlockSpec((MOBA_BLOCK, W_CKV), lambda n, b: (n * nb + b, OFF_KC // W_CKV)),
                  pl.BlockSpec((MOBA_BLOCK, W_CKV), lambda n, b: (n * nb + b, OFF_VC // W_CKV)),
                  pl.BlockSpec((MOBA_BLOCK, W_C), lambda n, b: (n * nb + b, OFF_ZC // W_C))],
        out_specs=pl.BlockSpec((MOBA_BLOCK, W_C), lambda n, b: (n * nb + b, 0)),
        out_shape=jax.ShapeDtypeStruct((n_seq * seq, W_C), BF16),
        scratch_shapes=[pltpu.VMEM((nb, MOBA_BLOCK, LANES), BF16),
                        pltpu.VMEM((nb, LANES, MOBA_BLOCK), BF16),
                        pltpu.VMEM((nb, LANES), F32),
                        pltpu.VMEM((cols, LANES), BF16),
                        pltpu.VMEM((nb, cols), F32),
                        pltpu.VMEM((1, cols), F32),
                        pltpu.VMEM((1, cols), F32),
                        pltpu.VMEM((LANES, cols), F32)],
        compiler_params=_params("parallel", "arbitrary"),
        name="moba_prompt",
    )(h, h, h, h)


def _moba_decode_kernel(pt_ref, q_ref, kn_ref, vn_ref, k_hbm, v_hbm, o_ref, k_buf, v_buf, sems, s_ref, p_ref,
                        *, layer, n_pages):
    slot = _paged_prefetch(pt_ref, layer, n_pages, (k_hbm, v_hbm), (k_buf, v_buf), sems)
    k_refs = [k_buf.at[slot, i] for i in range(n_pages)]
    v_refs = [v_buf.at[slot, i] for i in range(n_pages)]
    pages_per_block = MOBA_BLOCK // PAGE_SIZE
    n_blocks = n_pages // pages_per_block
    q = q_ref[...]
    lane = lax.broadcasted_iota(jnp.int32, (1, LANES), 1)

    gate = jnp.full((MOBA_HEADS, LANES), -jnp.inf, F32)
    for b in range(n_blocks):
        tot = jnp.zeros((MOBA_HEADS, PAGE_SIZE), F32)
        for i in range(pages_per_block):
            pg = b * pages_per_block + i
            s = _dot(q, k_refs[pg][...].astype(BF16))
            s_ref[pg] = s
            tot = tot + s
        gate = jnp.where(lane == b, jnp.sum(tot, axis=-1, keepdims=True), gate)
    picked = _topk_mask(gate, lane.astype(F32), MOBA_TOPK, axis=1)

    s_new = jnp.sum(q.astype(F32) * kn_ref[...], axis=-1, keepdims=True)
    m_lanes = jnp.full((MOBA_HEADS, PAGE_SIZE), NEG, F32)
    for b in range(n_blocks):
        keep = picked[:, b:b + 1] > 0.0
        for i in range(pages_per_block):
            pg = b * pages_per_block + i
            s = jnp.where(keep, s_ref[pg], NEG)
            s_ref[pg] = s
            m_lanes = jnp.maximum(m_lanes, s)
    m = jnp.maximum(jnp.max(m_lanes, axis=-1, keepdims=True), s_new)
    p_new = jnp.exp(s_new - m)
    l_lanes = jnp.zeros((MOBA_HEADS, PAGE_SIZE), F32)
    for pg in range(n_pages):
        p = jnp.exp(s_ref[pg] - m)
        l_lanes = l_lanes + p
        p_ref[pg] = p.astype(BF16)
    l = p_new + jnp.sum(l_lanes, axis=-1, keepdims=True)
    acc = p_new * vn_ref[...]
    for pg in range(n_pages):
        acc = acc + _dot_nt(p_ref[pg], v_refs[pg][...].astype(BF16))
    o = acc / l
    o_ref[...] = jnp.where(lane < MOBA_HEAD_DIM, o[:MOBA_GROUP], o[MOBA_GROUP:])


def _moba_decode(page_table, q, knew, vnew, cache_k_t, cache_v_t, layer):
    b, n_pages = page_table.shape
    grid_spec = pltpu.PrefetchScalarGridSpec(
        num_scalar_prefetch=1,
        grid=(b,),
        in_specs=[pl.BlockSpec((None, MOBA_HEADS, LANES), lambda s, pt: (s, 0, 0)),
                  pl.BlockSpec((None, 1, W_CKV), lambda s, pt: (s, 0, 0)),
                  pl.BlockSpec((None, 1, W_CKV), lambda s, pt: (s, 0, 0)),
                  pl.BlockSpec(memory_space=pl.ANY),
                  pl.BlockSpec(memory_space=pl.ANY)],
        out_specs=pl.BlockSpec((None, MOBA_GROUP, LANES), lambda s, pt: (s, 0, 0)),
        scratch_shapes=[pltpu.VMEM((2, n_pages, W_CKV, PAGE_SIZE), F32),
                        pltpu.VMEM((2, n_pages, W_CKV, PAGE_SIZE), F32),
                        pltpu.SemaphoreType.DMA((2, 2)),
                        pltpu.VMEM((n_pages, MOBA_HEADS, PAGE_SIZE), F32),
                        pltpu.VMEM((n_pages, MOBA_HEADS, PAGE_SIZE), BF16)])
    return pl.pallas_call(
        functools.partial(_moba_decode_kernel, layer=layer, n_pages=n_pages),
        grid_spec=grid_spec,
        out_shape=jax.ShapeDtypeStruct((b, MOBA_GROUP, LANES), F32),
        compiler_params=_params("arbitrary"),
        name="moba_decode",
    )(page_table, q, knew, vnew, cache_k_t, cache_v_t)


def _sample_out_kernel(ol_ref, oc_ref, zb_ref, zc_ref, wuv_ref, yb_ref, yc_ref):
    for h in range(MLA_HEADS):
        hs = slice(h * V_DIM, (h + 1) * V_DIM)
        y = _dot(ol_ref[h].astype(BF16), wuv_ref[h])
        yb_ref[:, hs] = (y * _silu(zb_ref[:, hs])).astype(BF16)
    yc_ref[...] = (oc_ref[...] * _silu(zc_ref[...])).astype(BF16)


def _sample_out(o_lat, o_c, h, wuv):
    b = h.shape[0]
    return pl.pallas_call(
        _sample_out_kernel,
        grid=(1,),
        in_specs=[pl.BlockSpec((MLA_HEADS, b, KV_LORA), lambda i: (0, 0, 0)),
                  pl.BlockSpec((b, W_C), lambda i: (0, 0)),
                  pl.BlockSpec((b, W_B), lambda i: (0, OFF_ZB // W_B)),
                  pl.BlockSpec((b, W_C), lambda i: (0, OFF_ZC // W_C)),
                  pl.BlockSpec((MLA_HEADS, KV_LORA, V_DIM), lambda i: (0, 0, 0))],
        out_specs=[pl.BlockSpec((b, W_B), lambda i: (0, 0)),
                   pl.BlockSpec((b, W_C), lambda i: (0, 0))],
        out_shape=[jax.ShapeDtypeStruct((b, W_B), BF16),
                   jax.ShapeDtypeStruct((b, W_C), BF16)],
        compiler_params=_params("arbitrary"),
        name="sample_out",
    )(o_lat, o_c, h, h, wuv)


def _merge_kernel(ya_ref, yb_ref, yc_ref, ga_ref, gb_ref, gc_ref, pa_ref, pb_ref, pc_ref, o_ref):
    m = (_sigmoid(ga_ref[...]) * _dot(ya_ref[...], pa_ref[...])
         + _sigmoid(gb_ref[...]) * _dot(yb_ref[...], pb_ref[...])
         + _sigmoid(gc_ref[...]) * _dot(yc_ref[...], pc_ref[...]))
    o_ref[...] = m.astype(BF16)


def _merge(ya, yb, yc, h, pa, pb, pc, tm, tn):
    r = ya.shape[0]
    gate_blk = lambda b: (lambda i, j: (i, (OFF_G + b * D_MODEL) // tn + j))
    row = lambda i, j: (i, 0)
    col = lambda i, j: (0, j)
    return pl.pallas_call(
        _merge_kernel,
        grid=(r // tm, D_MODEL // tn),
        in_specs=[pl.BlockSpec((tm, W_A), row), pl.BlockSpec((tm, W_B), row), pl.BlockSpec((tm, W_C), row),
                  pl.BlockSpec((tm, tn), gate_blk(0)), pl.BlockSpec((tm, tn), gate_blk(1)),
                  pl.BlockSpec((tm, tn), gate_blk(2)),
                  pl.BlockSpec((W_A, tn), col), pl.BlockSpec((W_B, tn), col), pl.BlockSpec((W_C, tn), col)],
        out_specs=pl.BlockSpec((tm, tn), lambda i, j: (i, j)),
        out_shape=jax.ShapeDtypeStruct((r, D_MODEL), BF16),
        compiler_params=_params("parallel", "arbitrary"),
        name="merge",
    )(ya, yb, yc, h, h, h, pa, pb, pc)


def _outproj_kernel(m_ref, w_ref, x_ref, o_ref):
    o_ref[...] = x_ref[...] + _dot(m_ref[...], w_ref[...])


def _outproj(m, w, x, tm, tn):
    r = m.shape[0]
    return pl.pallas_call(
        _outproj_kernel,
        grid=(r // tm, D_MODEL // tn),
        in_specs=[pl.BlockSpec((tm, D_MODEL), lambda i, j: (i, 0)),
                  pl.BlockSpec((D_MODEL, tn), lambda i, j: (0, j)),
                  pl.BlockSpec((tm, tn), lambda i, j: (i, j))],
        out_specs=pl.BlockSpec((tm, tn), lambda i, j: (i, j)),
        out_shape=jax.ShapeDtypeStruct((r, D_MODEL), F32),
        compiler_params=_params("parallel", "arbitrary"),
        name="outproj",
    )(m, w, x)


def _final_norm_kernel(x_ref, g_ref, o_ref):
    x = x_ref[...]
    ms = jnp.mean(x * x, axis=-1, keepdims=True)
    o_ref[...] = x * lax.rsqrt(ms + EPS) * g_ref[...]


def _final_norm(x, g, tm):
    r = x.shape[0]
    return pl.pallas_call(
        _final_norm_kernel,
        grid=(r // tm,),
        in_specs=[pl.BlockSpec((tm, D_MODEL), lambda i: (i, 0)),
                  pl.BlockSpec((1, D_MODEL), lambda i: (0, 0))],
        out_specs=pl.BlockSpec((tm, D_MODEL), lambda i: (i, 0)),
        out_shape=jax.ShapeDtypeStruct((r, D_MODEL), F32),
        compiler_params=_params("parallel"),
        name="final_norm",
    )(x, g)


def _pair_heads(w):
    lead = w.shape[:-1]
    w = w.reshape(lead + (MOBA_KV_HEADS, MOBA_GROUP, MOBA_HEAD_DIM))
    return jnp.swapaxes(w, -3, -2).reshape(lead + (W_C,))


def _swap_halves(w):
    half = ROPE_DIM // 2
    return jnp.concatenate([w[..., half:], w[..., :half]], axis=-1)


def _prep_weights(w_in, w_pool, w_uq, w_uk, w_uv, p_a, p_b, p_c, w_out):
    offs = np.cumsum((0,) + IN_SPLITS)
    w_in_t = jnp.swapaxes(w_in, 1, 2)
    piece = lambda i: w_in_t[:, offs[i]:offs[i + 1], :]
    u_a, z_a, c_q, c_kv, k_pe, z_b, q_c, k_c, v_c, z_c, g = [piece(i) for i in range(len(IN_SPLITS))]
    pair_rows = lambda t: jnp.swapaxes(_pair_heads(jnp.swapaxes(t, 1, 2)), 1, 2)
    swap_rows = lambda t: jnp.swapaxes(_swap_halves(jnp.swapaxes(t, 1, 2)), 1, 2)
    w_in_p = jnp.concatenate(
        [z_b, u_a, z_a, pair_rows(q_c), pair_rows(z_c), g, c_q, c_kv, k_c, v_c, k_pe, swap_rows(k_pe)],
        axis=1).astype(BF16)
    wq = w_uq.reshape(DEPTH, Q_LORA, MLA_HEADS, NOPE_DIM + ROPE_DIM)
    wqn = wq[..., :NOPE_DIM].reshape(DEPTH, Q_LORA, MLA_HEADS * NOPE_DIM).astype(BF16)
    pe = wq[..., NOPE_DIM:]
    wqp = jnp.concatenate([pe, _swap_halves(pe)], axis=-1).reshape(DEPTH, Q_LORA, MLA_HEADS * LANES).astype(BF16)
    wuk = jnp.transpose(w_uk, (0, 2, 3, 1)).astype(BF16)
    wuv = jnp.transpose(w_uv, (0, 2, 1, 3)).astype(BF16)
    p_c_p = jnp.swapaxes(p_c.reshape(DEPTH, MOBA_KV_HEADS, MOBA_GROUP, MOBA_HEAD_DIM, D_MODEL), 1, 2)
    p_c_p = p_c_p.reshape(DEPTH, W_C, D_MODEL).astype(BF16)
    return dict(w_in=w_in_p, w_pool=w_pool.astype(BF16), wqn=wqn, wqp=wqp, wuk=wuk, wuv=wuv,
                p_a=p_a.astype(BF16), p_b=p_b.astype(BF16), p_c=p_c_p, w_out=w_out.astype(BF16))


def _rope_table(pos):
    half = ROPE_DIM // 2
    inv = ROPE_THETA ** (-jnp.arange(half, dtype=F32) / half)
    ang = pos.astype(F32)[:, None] * inv[None, :]
    cos, sin = jnp.cos(ang), jnp.sin(ang)
    return jnp.concatenate([cos, cos, -sin, sin], axis=-1)


def _layer_prompt(x, w, l, cs, norm_g, pool_scale, q_norm_g, kv_norm_g, n_seq, seq):
    h = _inproj(x, norm_g[l][None], w["w_in"][l], tm=1024, tn=896)
    ya = _pool_prompt(h, w["w_pool"][l], pool_scale[l][None], n_seq, seq)
    q, kcat, ckv_t, ckv_new, kpe_new = _mla_prep(h, cs, q_norm_g[l][None], kv_norm_g[l][None],
                                                 w["wqn"][l], w["wqp"][l], w["wuk"][l], tm=256)
    yb = _mla_prompt(q, kcat, ckv_t, h, w["wuv"][l], n_seq, seq)
    yc = _moba_prompt(h, n_seq, seq)
    m = _merge(ya, yb, yc, h, w["p_a"][l], w["p_b"][l], w["p_c"][l], tm=1024, tn=512)
    x = _outproj(m, w["w_out"][l], x, tm=1024, tn=512)
    state = (h[:, OFF_UA:OFF_UA + W_A].reshape(n_seq, seq, W_A)[:, seq - POOL_HIST:],
             ckv_new.reshape(n_seq, seq, KV_LORA),
             kpe_new.reshape(n_seq, seq, ROPE_DIM),
             h[:, OFF_KC:OFF_KC + W_CKV].reshape(n_seq, seq, MOBA_KV_HEADS, MOBA_HEAD_DIM),
             h[:, OFF_VC:OFF_VC + W_CKV].reshape(n_seq, seq, MOBA_KV_HEADS, MOBA_HEAD_DIM))
    return x, state


def _layer_sample(x, w, l, cs, norm_g, pool_scale, q_norm_g, kv_norm_g,
                  hist2d, cache_ckv, cache_kpe_t, cache_k_t, cache_v_t, page_table):
    b = x.shape[0]
    h = _inproj(x, norm_g[l][None], w["w_in"][l], tm=b, tn=1920)
    ya, pool_new = _pool_sample(hist2d, h, w["w_pool"][l], pool_scale[l][None])
    q, kcat, _, ckv_new, kpe_new = _mla_prep(h, cs, q_norm_g[l][None], kv_norm_g[l][None],
                                             w["wqn"][l], w["wqp"][l], w["wuk"][l], tm=b)
    o_lat = _mla_decode(page_table, jnp.swapaxes(q, 0, 1), kcat[:, None, :], cache_ckv, cache_kpe_t, l)
    qc = h[:, OFF_QC:OFF_QC + W_C].reshape(b, 1, MOBA_GROUP, LANES) * MOBA_SCALE
    half = (jnp.arange(LANES)[None, None, None, :] // MOBA_HEAD_DIM
            == jnp.arange(MOBA_KV_HEADS)[None, :, None, None])
    qd = jnp.where(half, qc, 0.0).reshape(b, MOBA_HEADS, LANES).astype(BF16)
    k_new = h[:, OFF_KC:OFF_KC + W_CKV]
    v_new = h[:, OFF_VC:OFF_VC + W_CKV]
    o_c = _moba_decode(page_table, qd, k_new[:, None, :], v_new[:, None, :], cache_k_t, cache_v_t, l)
    yb, yc = _sample_out(jnp.swapaxes(o_lat, 0, 1), o_c.reshape(b, W_C), h, w["wuv"][l])
    m = _merge(ya, yb, yc, h, w["p_a"][l], w["p_b"][l], w["p_c"][l], tm=b, tn=512)
    x = _outproj(m, w["w_out"][l], x, tm=b, tn=512)
    state = (pool_new.reshape(b, POOL_HIST, W_A),
             ckv_new.reshape(b, 1, KV_LORA),
             kpe_new.reshape(b, 1, ROPE_DIM),
             k_new.reshape(b, 1, MOBA_KV_HEADS, MOBA_HEAD_DIM),
             v_new.reshape(b, 1, MOBA_KV_HEADS, MOBA_HEAD_DIM))
    return x, state


@jax.jit
def _forward(x_prompt, x_sample, state_pool, cache_ckv, cache_kpe, cache_k, cache_v, page_table,
             norm_g, w_in, w_pool, pool_scale, q_norm_g, kv_norm_g, w_uq, w_uk, w_uv,
             p_a, p_b, p_c, w_out, final_g):
    n_seq, seq, _ = x_prompt.shape
    b = x_sample.shape[0]
    n_pool = cache_k.shape[1]
    past_len = page_table.shape[1] * PAGE_SIZE
    w = _prep_weights(w_in, w_pool, w_uq, w_uk, w_uv, p_a, p_b, p_c, w_out)
    cs_p = jnp.tile(_rope_table(jnp.arange(seq)), (n_seq, 1))
    cs_s = jnp.tile(_rope_table(jnp.full((1,), past_len)), (b, 1))
    cache_kpe_t = jnp.swapaxes(cache_kpe, 2, 3)
    cache_k_t = jnp.transpose(cache_k, (0, 1, 3, 4, 2)).reshape(DEPTH, n_pool, W_CKV, PAGE_SIZE)
    cache_v_t = jnp.transpose(cache_v, (0, 1, 3, 4, 2)).reshape(DEPTH, n_pool, W_CKV, PAGE_SIZE)
    hist2d = state_pool.reshape(DEPTH, b, POOL_HIST * W_A)

    xp = x_prompt.reshape(n_seq * seq, D_MODEL)
    xs = x_sample.reshape(b, D_MODEL)
    new_p, new_s = [], []
    for l in range(DEPTH):
        xs, st_s = _layer_sample(xs, w, l, cs_s, norm_g, pool_scale, q_norm_g, kv_norm_g,
                                 hist2d[l], cache_ckv, cache_kpe_t, cache_k_t, cache_v_t, page_table)
        xp, st_p = _layer_prompt(xp, w, l, cs_p, norm_g, pool_scale, q_norm_g, kv_norm_g, n_seq, seq)
        new_p.append(st_p)
        new_s.append(st_s)
    y_prompt = _final_norm(xp, final_g[None], tm=1024).reshape(n_seq, seq, D_MODEL)
    y_sample = _final_norm(xs, final_g[None], tm=b).reshape(b, 1, D_MODEL)
    stack = lambda sts, i: jnp.stack([st[i] for st in sts])
    return ((y_prompt, y_sample)
            + tuple(stack(new_p, i) for i in range(5))
            + tuple(stack(new_s, i) for i in range(5)))


def kernel(x_prompt, x_sample, state_pool, cache_ckv, cache_kpe, cache_k, cache_v, page_table,
           norm_g, w_in, w_pool, pool_scale, q_norm_g, kv_norm_g, w_uq, w_uk, w_uv,
           p_a, p_b, p_c, w_out, final_g):
    return _forward(x_prompt, x_sample, state_pool, cache_ckv, cache_kpe, cache_k, cache_v, page_table,
                    norm_g, w_in, w_pool, pool_scale, q_norm_g, kv_norm_g, w_uq, w_uk, w_uv,
                    p_a, p_b, p_c, w_out, final_g)
```

```python
import functools

import jax
import jax.numpy as jnp
import numpy as np
from jax import lax
from jax.experimental import pallas as pl
from jax.experimental.pallas import tpu as pltpu

F32 = jnp.float32
BF16 = jnp.bfloat16

D_MODEL = 2048
DEPTH = 4
PAGE_SIZE = 128
POOL_WINDOWS = (2, 4, 8, 16)
POOL_GROUP = 256
W_A = len(POOL_WINDOWS) * POOL_GROUP
POOL_HIST = max(POOL_WINDOWS) - 1
MLA_HEADS = 16
Q_LORA = 512
KV_LORA = 256
NOPE_DIM = 128
ROPE_DIM = 64
V_DIM = 128
W_B = MLA_HEADS * V_DIM
ROPE_THETA = 10000.0
MOBA_HEADS = 16
MOBA_KV_HEADS = 2
MOBA_GROUP = MOBA_HEADS // MOBA_KV_HEADS
MOBA_HEAD_DIM = 64
MOBA_BLOCK = 256
MOBA_TOPK = 3
W_C = MOBA_HEADS * MOBA_HEAD_DIM
W_CKV = MOBA_KV_HEADS * MOBA_HEAD_DIM
N_BRANCH = 3
EPS = 1e-6
NEG = -1e30
IN_SPLITS = (W_A, W_A, Q_LORA, KV_LORA, ROPE_DIM, W_B, W_C, W_CKV, W_CKV, W_C, N_BRANCH * D_MODEL)

LOG2_E = 1.4426950408889634
MLA_SCALE = (NOPE_DIM + ROPE_DIM) ** -0.5 * LOG2_E
MOBA_SCALE = MOBA_HEAD_DIM ** -0.5 * LOG2_E
LANES = 128
QK_WIDTH = KV_LORA + LANES
DECODE_SEQS = 2

OFF_ZB = 0
OFF_UA = OFF_ZB + W_B
OFF_ZA = OFF_UA + W_A
OFF_QC = OFF_ZA + W_A
OFF_ZC = OFF_QC + W_C
OFF_G = OFF_ZC + W_C
OFF_CQ = OFF_G + N_BRANCH * D_MODEL
OFF_CKV = OFF_CQ + Q_LORA
OFF_KC = OFF_CKV + KV_LORA
OFF_VC = OFF_KC + W_CKV
OFF_KPE = OFF_VC + W_CKV
N_H = OFF_KPE + 2 * ROPE_DIM

VMEM_LIMIT = 56 * 1024 * 1024


def _params(*sem):
    return pltpu.CompilerParams(dimension_semantics=sem, vmem_limit_bytes=VMEM_LIMIT)


def _silu(z):
    return z * (1.0 / (1.0 + jnp.exp(-z)))


def _sigmoid(z):
    return 1.0 / (1.0 + jnp.exp(-z))


def _dot(a, b):
    return jnp.dot(a, b, preferred_element_type=F32)


def _dot_nt(a, b):
    return lax.dot_general(a, b, (((1,), (1,)), ((), ())), preferred_element_type=F32)


def _inproj_kernel(x_ref, g_ref, w_ref, o_ref, xn_ref):
    @pl.when(pl.program_id(1) == 0)
    def _():
        x = x_ref[...]
        ms = jnp.mean(x * x, axis=-1, keepdims=True)
        xn_ref[...] = (x * lax.rsqrt(ms + EPS) * g_ref[...]).astype(BF16)

    o_ref[...] = _dot_nt(xn_ref[...], w_ref[...])


def _inproj(x, g, w, tm, tn):
    r, n = x.shape[0], w.shape[0]
    return pl.pallas_call(
        _inproj_kernel,
        grid=(r // tm, n // tn),
        in_specs=[pl.BlockSpec((tm, D_MODEL), lambda i, j: (i, 0)),
                  pl.BlockSpec((1, D_MODEL), lambda i, j: (0, 0)),
                  pl.BlockSpec((tn, D_MODEL), lambda i, j: (j, 0))],
        out_specs=pl.BlockSpec((tm, tn), lambda i, j: (i, j)),
        out_shape=jax.ShapeDtypeStruct((r, n), F32),
        scratch_shapes=[pltpu.VMEM((tm, D_MODEL), BF16)],
        compiler_params=_params("parallel", "arbitrary"),
        name="inproj",
    )(x, g, w)


def _pool_finish(win_sum, cnt, u, z, wp_ref, sc_ref, g):
    cols = slice(g * POOL_GROUP, (g + 1) * POOL_GROUP)
    d = (win_sum / cnt - u).astype(BF16)
    y = _dot(d, wp_ref[g]) * sc_ref[:, cols]
    return (y * _silu(z)).astype(BF16)


def _pool_prompt_kernel(u_ref, z_ref, wp_ref, sc_ref, o_ref, ext_ref, *, ts):
    t = pl.program_id(1)
    halo = POOL_HIST + 1

    @pl.when(t == 0)
    def _():
        ext_ref[0:halo, :] = jnp.zeros((halo, W_A), F32)

    @pl.when(t > 0)
    def _():
        ext_ref[0:halo, :] = ext_ref[ts:ts + halo, :]

    ext_ref[halo:, :] = u_ref[...]
    pos = t * ts + lax.broadcasted_iota(jnp.int32, (ts, 1), 0)
    for g, w in enumerate(POOL_WINDOWS):
        cols = slice(g * POOL_GROUP, (g + 1) * POOL_GROUP)
        u = u_ref[:, cols]
        win_sum = u
        for k in range(1, w):
            win_sum = win_sum + ext_ref[halo - k:halo - k + ts, cols]
        cnt = jnp.minimum(w, pos + 1).astype(F32)
        o_ref[:, cols] = _pool_finish(win_sum, cnt, u, z_ref[:, cols], wp_ref, sc_ref, g)


def _pool_prompt(h, wp, sc, n_seq, seq, ts=512):
    nt = seq // ts
    blk = W_A
    return pl.pallas_call(
        functools.partial(_pool_prompt_kernel, ts=ts),
        grid=(n_seq, nt),
        in_specs=[pl.BlockSpec((ts, blk), lambda n, t: (n * nt + t, OFF_UA // blk)),
                  pl.BlockSpec((ts, blk), lambda n, t: (n * nt + t, OFF_ZA // blk)),
                  pl.BlockSpec((len(POOL_WINDOWS), POOL_GROUP, POOL_GROUP), lambda n, t: (0, 0, 0)),
                  pl.BlockSpec((1, W_A), lambda n, t: (0, 0))],
        out_specs=pl.BlockSpec((ts, blk), lambda n, t: (n * nt + t, 0)),
        out_shape=jax.ShapeDtypeStruct((n_seq * seq, W_A), BF16),
        scratch_shapes=[pltpu.VMEM((POOL_HIST + 1 + ts, W_A), F32)],
        compiler_params=_params("parallel", "arbitrary"),
        name="pool_prompt",
    )(h, h, wp, sc)


def _pool_sample_kernel(hist_ref, u_ref, z_ref, wp_ref, sc_ref, o_ref, st_ref):
    for g, w in enumerate(POOL_WINDOWS):
        cols = slice(g * POOL_GROUP, (g + 1) * POOL_GROUP)
        u = u_ref[:, cols]
        win_sum = u
        for k in range(1, w):
            off = (POOL_HIST - k) * W_A + g * POOL_GROUP
            win_sum = win_sum + hist_ref[:, off:off + POOL_GROUP]
        o_ref[:, cols] = _pool_finish(win_sum, float(w), u, z_ref[:, cols], wp_ref, sc_ref, g)
    st_ref[:, :(POOL_HIST - 1) * W_A] = hist_ref[:, W_A:]
    st_ref[:, (POOL_HIST - 1) * W_A:] = u_ref[...]


def _pool_sample(hist2d, h, wp, sc):
    b = h.shape[0]
    blk = W_A
    return pl.pallas_call(
        _pool_sample_kernel,
        grid=(1,),
        in_specs=[pl.BlockSpec((b, POOL_HIST * W_A), lambda i: (0, 0)),
                  pl.BlockSpec((b, blk), lambda i: (0, OFF_UA // blk)),
                  pl.BlockSpec((b, blk), lambda i: (0, OFF_ZA // blk)),
                  pl.BlockSpec((len(POOL_WINDOWS), POOL_GROUP, POOL_GROUP), lambda i: (0, 0, 0)),
                  pl.BlockSpec((1, W_A), lambda i: (0, 0))],
        out_specs=[pl.BlockSpec((b, W_A), lambda i: (0, 0)),
                   pl.BlockSpec((b, POOL_HIST * W_A), lambda i: (0, 0))],
        out_shape=[jax.ShapeDtypeStruct((b, W_A), BF16),
                   jax.ShapeDtypeStruct((b, POOL_HIST * W_A), F32)],
        compiler_params=_params("arbitrary"),
        name="pool_sample",
    )(hist2d, h, h, wp, sc)


def _mla_prep_kernel(cq_ref, ckv_ref, kpe_ref, cs_ref, gq_ref, gkv_ref, wqn_ref, wqp_ref, wuk_ref,
                     q_ref, kcat_ref, ckvt_ref, ckvn_ref, kper_ref):
    lane = lax.broadcasted_iota(jnp.int32, (1, LANES), 1)
    first_half = lane < ROPE_DIM
    cs = cs_ref[...]

    cq = cq_ref[...]
    ms = jnp.mean(cq * cq, axis=-1, keepdims=True)
    cqn = (cq * lax.rsqrt(ms + EPS) * gq_ref[...]).astype(BF16)
    qn = _dot(cqn, wqn_ref[...])
    qp = _dot(cqn, wqp_ref[...])
    t = qp * jnp.concatenate([cs] * MLA_HEADS, axis=1)
    rot = t + pltpu.roll(t, MLA_HEADS * LANES - ROPE_DIM, axis=1)
    for h in range(MLA_HEADS):
        hs = slice(h * LANES, (h + 1) * LANES)
        ql = _dot(qn[:, hs].astype(BF16), wuk_ref[h]) * MLA_SCALE
        q_ref[h, :, :KV_LORA] = ql.astype(BF16)
        q_ref[h, :, KV_LORA:] = jnp.where(first_half, rot[:, hs] * MLA_SCALE, 0.0).astype(BF16)

    ckv = ckv_ref[...]
    ms = jnp.mean(ckv * ckv, axis=-1, keepdims=True)
    ckvn = ckv * lax.rsqrt(ms + EPS) * gkv_ref[...]
    ckvn_ref[...] = ckvn
    kcat_ref[:, :KV_LORA] = ckvn.astype(BF16)
    ckvt_ref[...] = ckvn.T.astype(BF16)
    tk = kpe_ref[...] * cs
    krot = tk + pltpu.roll(tk, ROPE_DIM, axis=1)
    kper_ref[...] = krot[:, :ROPE_DIM]
    kcat_ref[:, KV_LORA:] = jnp.where(first_half, krot, 0.0).astype(BF16)


def _mla_prep(h, cs, gq, gkv, wqn, wqp, wuk, tm):
    r = h.shape[0]
    const2 = lambda i: (0, 0)
    return pl.pallas_call(
        _mla_prep_kernel,
        grid=(r // tm,),
        in_specs=[pl.BlockSpec((tm, Q_LORA), lambda i: (i, OFF_CQ // Q_LORA)),
                  pl.BlockSpec((tm, KV_LORA), lambda i: (i, OFF_CKV // KV_LORA)),
                  pl.BlockSpec((tm, LANES), lambda i: (i, OFF_KPE // LANES)),
                  pl.BlockSpec((tm, LANES), lambda i: (i, 0)),
                  pl.BlockSpec((1, Q_LORA), const2),
                  pl.BlockSpec((1, KV_LORA), const2),
                  pl.BlockSpec((Q_LORA, MLA_HEADS * NOPE_DIM), const2),
                  pl.BlockSpec((Q_LORA, MLA_HEADS * LANES), const2),
                  pl.BlockSpec((MLA_HEADS, NOPE_DIM, KV_LORA), lambda i: (0, 0, 0))],
        out_specs=[pl.BlockSpec((MLA_HEADS, tm, QK_WIDTH), lambda i: (0, i, 0)),
                   pl.BlockSpec((tm, QK_WIDTH), lambda i: (i, 0)),
                   pl.BlockSpec((KV_LORA, tm), lambda i: (0, i)),
                   pl.BlockSpec((tm, KV_LORA), lambda i: (i, 0)),
                   pl.BlockSpec((tm, ROPE_DIM), lambda i: (i, 0))],
        out_shape=[jax.ShapeDtypeStruct((MLA_HEADS, r, QK_WIDTH), BF16),
                   jax.ShapeDtypeStruct((r, QK_WIDTH), BF16),
                   jax.ShapeDtypeStruct((KV_LORA, r), BF16),
                   jax.ShapeDtypeStruct((r, KV_LORA), F32),
                   jax.ShapeDtypeStruct((r, ROPE_DIM), F32)],
        compiler_params=_params("parallel"),
        name="mla_prep",
    )(h, h, h, cs, gq, gkv, wqn, wqp, wuk)


def _mla_prompt_kernel(qi_ref, ki_ref, q_ref, k_ref, kt_ref, zb_ref, wuv_ref, o_ref, m_ref, l_ref, acc_ref,
                       *, tq, tk):
    qi = qi_ref[pl.program_id(1)]
    ki = ki_ref[pl.program_id(1)]
    last = (qi * tq + tq - 1) // tk
    cols = MLA_HEADS * tq

    @pl.when(ki == 0)
    def _():
        m_ref[...] = jnp.full((1, cols), NEG, F32)
        l_ref[...] = jnp.zeros((1, cols), F32)
        acc_ref[...] = jnp.zeros((KV_LORA, cols), F32)

    def update(masked):
        q = q_ref[...].reshape(cols, QK_WIDTH)
        s = _dot_nt(k_ref[...], q)
        if masked:
            qpos = qi * tq + (lax.broadcasted_iota(jnp.int32, (1, cols), 1) & (tq - 1))
            kpos = ki * tk + lax.broadcasted_iota(jnp.int32, (tk, 1), 0)
            s = jnp.where(kpos <= qpos, s, NEG)
        m_prev = m_ref[...]
        m_new = jnp.maximum(m_prev, jnp.max(s, axis=0, keepdims=True))
        alpha = jnp.exp2(m_prev - m_new)
        p = jnp.exp2(s - m_new)
        l_ref[...] = alpha * l_ref[...] + jnp.sum(p, axis=0, keepdims=True)
        acc_ref[...] = alpha * acc_ref[...] + _dot(kt_ref[...], p.astype(BF16))
        m_ref[...] = m_new

    @pl.when(ki < last)
    def _():
        update(False)

    @pl.when(ki == last)
    def _():
        update(True)
        o_t = acc_ref[...] / l_ref[...]
        for h in range(MLA_HEADS):
            hs = slice(h * V_DIM, (h + 1) * V_DIM)
            o_h = o_t[:, h * tq:(h + 1) * tq].T.astype(BF16)
            y = _dot(o_h, wuv_ref[h])
            o_ref[:, hs] = (y * _silu(zb_ref[:, hs])).astype(BF16)


def _mla_prompt(q, kcat, ckv_t, h, wuv, n_seq, seq, tq=128, tk=512):
    nq, nk = seq // tq, seq // tk
    pairs = [(i, j) for i in range(nq) for j in range((i * tq + tq - 1) // tk + 1)]
    qi_tab = jnp.asarray([p[0] for p in pairs], jnp.int32)
    ki_tab = jnp.asarray([p[1] for p in pairs], jnp.int32)
    grid_spec = pltpu.PrefetchScalarGridSpec(
        num_scalar_prefetch=2,
        grid=(n_seq, len(pairs)),
        in_specs=[pl.BlockSpec((MLA_HEADS, tq, QK_WIDTH), lambda n, t, qi, ki: (0, n * nq + qi[t], 0)),
                  pl.BlockSpec((tk, QK_WIDTH), lambda n, t, qi, ki: (n * nk + ki[t], 0)),
                  pl.BlockSpec((KV_LORA, tk), lambda n, t, qi, ki: (0, n * nk + ki[t])),
                  pl.BlockSpec((tq, W_B), lambda n, t, qi, ki: (n * nq + qi[t], OFF_ZB // W_B)),
                  pl.BlockSpec((MLA_HEADS, KV_LORA, V_DIM), lambda n, t, qi, ki: (0, 0, 0))],
        out_specs=pl.BlockSpec((tq, W_B), lambda n, t, qi, ki: (n * nq + qi[t], 0)),
        scratch_shapes=[pltpu.VMEM((1, MLA_HEADS * tq), F32),
                        pltpu.VMEM((1, MLA_HEADS * tq), F32),
                        pltpu.VMEM((KV_LORA, MLA_HEADS * tq), F32)])
    return pl.pallas_call(
        functools.partial(_mla_prompt_kernel, tq=tq, tk=tk),
        grid_spec=grid_spec,
        out_shape=jax.ShapeDtypeStruct((n_seq * seq, W_B), BF16),
        compiler_params=_params("parallel", "arbitrary"),
        name="mla_prompt",
    )(qi_tab, ki_tab, q, kcat, ckv_t, h, wuv)


def _page_copies(pt_ref, step, slot, layer, n_pages, caches, bufs, sems):
    copies = []
    for u in range(DECODE_SEQS):
        for i in range(n_pages):
            page = pt_ref[step * DECODE_SEQS + u, i]
            for a, (cache, buf) in enumerate(zip(caches, bufs)):
                copies.append(pltpu.make_async_copy(
                    cache.at[layer, page], buf.at[slot * DECODE_SEQS + u, i], sems.at[a, slot]))
    return copies


def _paged_prefetch(pt_ref, layer, n_pages, caches, bufs, sems):
    step = pl.program_id(0)
    slot = step % 2

    @pl.when(step == 0)
    def _():
        for c in _page_copies(pt_ref, step, slot, layer, n_pages, caches, bufs, sems):
            c.start()

    @pl.when(step + 1 < pl.num_programs(0))
    def _():
        for c in _page_copies(pt_ref, step + 1, 1 - slot, layer, n_pages, caches, bufs, sems):
            c.start()

    for c in _page_copies(pt_ref, step, slot, layer, n_pages, caches, bufs, sems):
        c.wait()
    return slot


def _mla_decode_one(q, kn, ckv_refs, kpe_refs, s_ref, p_ref):
    n_pages = len(ckv_refs)
    ql = q[:, :KV_LORA]
    qp = q[:, KV_LORA:KV_LORA + ROPE_DIM]
    s_new = jnp.sum(q.astype(F32) * kn, axis=-1, keepdims=True)
    m_lanes = jnp.full((MLA_HEADS, PAGE_SIZE), NEG, F32)
    for i in range(n_pages):
        s = _dot_nt(ql, ckv_refs[i][...].astype(BF16)) + _dot(qp, kpe_refs[i][...].astype(BF16))
        s_ref[i] = s
        m_lanes = jnp.maximum(m_lanes, s)
    m = jnp.maximum(jnp.max(m_lanes, axis=-1, keepdims=True), s_new)
    p_new = jnp.exp2(s_new - m)
    l_lanes = jnp.zeros((MLA_HEADS, PAGE_SIZE), F32)
    for i in range(n_pages):
        p = jnp.exp2(s_ref[i] - m)
        l_lanes = l_lanes + p
        p_ref[i] = p.astype(BF16)
    l = p_new + jnp.sum(l_lanes, axis=-1, keepdims=True)
    acc = p_new * kn[:, :KV_LORA]
    for i in range(n_pages):
        acc = acc + _dot(p_ref[i], ckv_refs[i][...].astype(BF16))
    return acc / l


def _mla_decode_kernel(pt_ref, q_ref, kn_ref, ckv_hbm, kpe_hbm, o_ref, ckv_buf, kpe_buf, sems, s_ref, p_ref,
                       *, layer, n_pages):
    slot = _paged_prefetch(pt_ref, layer, n_pages, (ckv_hbm, kpe_hbm), (ckv_buf, kpe_buf), sems)
    for u in range(DECODE_SEQS):
        b = slot * DECODE_SEQS + u
        o_ref[u] = _mla_decode_one(q_ref[u], kn_ref[u].astype(F32),
                                   [ckv_buf.at[b, i] for i in range(n_pages)],
                                   [kpe_buf.at[b, i] for i in range(n_pages)], s_ref.at[u], p_ref.at[u])


def _mla_decode(page_table, q, knew, cache_ckv, cache_kpe_t, layer):
    b, n_pages = page_table.shape
    u = DECODE_SEQS
    grid_spec = pltpu.PrefetchScalarGridSpec(
        num_scalar_prefetch=1,
        grid=(b // u,),
        in_specs=[pl.BlockSpec((u, MLA_HEADS, QK_WIDTH), lambda s, pt: (s, 0, 0)),
                  pl.BlockSpec((u, 1, QK_WIDTH), lambda s, pt: (s, 0, 0)),
                  pl.BlockSpec(memory_space=pl.ANY),
                  pl.BlockSpec(memory_space=pl.ANY)],
        out_specs=pl.BlockSpec((u, MLA_HEADS, KV_LORA), lambda s, pt: (s, 0, 0)),
        scratch_shapes=[pltpu.VMEM((2 * u, n_pages, PAGE_SIZE, KV_LORA), F32),
                        pltpu.VMEM((2 * u, n_pages, ROPE_DIM, PAGE_SIZE), F32),
                        pltpu.SemaphoreType.DMA((2, 2)),
                        pltpu.VMEM((u, n_pages, MLA_HEADS, PAGE_SIZE), F32),
                        pltpu.VMEM((u, n_pages, MLA_HEADS, PAGE_SIZE), BF16)])
    return pl.pallas_call(
        functools.partial(_mla_decode_kernel, layer=layer, n_pages=n_pages),
        grid_spec=grid_spec,
        out_shape=jax.ShapeDtypeStruct((b, MLA_HEADS, KV_LORA), F32),
        compiler_params=_params("arbitrary"),
        name="mla_decode",
    )(page_table, q, knew, cache_ckv, cache_kpe_t)


def _topk_mask(gate, idx_f, n_keep, axis):
    picked = jnp.zeros(gate.shape, F32)
    for r in range(MOBA_TOPK):
        mx = jnp.max(gate, axis=axis, keepdims=True)
        idx = jnp.min(jnp.where(gate == mx, idx_f, 1e9), axis=axis, keepdims=True)
        hit = idx_f == idx
        keep = jnp.where(r < n_keep, 1.0, 0.0)
        picked = jnp.maximum(picked, jnp.where(hit, keep, 0.0))
        gate = jnp.where(hit, -jnp.inf, gate)
    return picked


def _moba_prompt_kernel(q_ref, k_ref, v_ref, zc_ref, o_ref,
                        kb_ref, vt_ref, km_ref, qg_ref, bias_ref, m_ref, l_ref, acc_ref):
    qb = pl.program_id(1)
    blk = MOBA_BLOCK
    cols = MOBA_GROUP * blk
    n_blocks = km_ref.shape[0]
    lane = lax.broadcasted_iota(jnp.int32, (1, LANES), 1)

    @pl.when(qb == 0)
    def _():
        km_ref[...] = jnp.zeros(km_ref.shape, F32)

    k_own = k_ref[...]
    kb = k_own.astype(BF16)
    vt = v_ref[...].T.astype(BF16)
    kb_ref[qb] = kb
    vt_ref[qb] = vt
    kmean = km_ref[...].astype(BF16)
    km_ref[pl.ds(qb, 1), :] = jnp.sum(k_own, axis=0, keepdims=True) * (1.0 / blk)

    qf = q_ref[...]
    blk_idx = lax.broadcasted_iota(jnp.int32, (n_blocks, 1), 0)
    key_idx = lax.broadcasted_iota(jnp.int32, (blk, 1), 0)
    q_idx = lax.broadcasted_iota(jnp.int32, (1, cols), 1) & (blk - 1)
    causal = key_idx <= q_idx

    outs = []
    for c in range(MOBA_KV_HEADS):
        mine = (lane >= c * MOBA_HEAD_DIM) & (lane < (c + 1) * MOBA_HEAD_DIM)
        qg_ref[...] = (jnp.concatenate(
            [jnp.where(mine, qf[:, g * LANES:(g + 1) * LANES], 0.0) for g in range(MOBA_GROUP)],
            axis=0) * MOBA_SCALE).astype(BF16)
        gate = _dot_nt(kmean, qg_ref[...])
        picked = _topk_mask(jnp.where(blk_idx < qb, gate, -jnp.inf), blk_idx.astype(F32), qb, axis=0)
        bias_ref[...] = jnp.where(picked > 0.0, 0.0, NEG)

        s = jnp.where(causal, _dot_nt(kb, qg_ref[...]), NEG)
        m0 = jnp.max(s, axis=0, keepdims=True)
        p = jnp.exp2(s - m0)
        m_ref[...] = m0
        l_ref[...] = jnp.sum(p, axis=0, keepdims=True)
        acc_ref[...] = _dot(vt, p.astype(BF16))

        def past_block(j):
            s = _dot_nt(kb_ref[j], qg_ref[...])
            bias = bias_ref[pl.ds(j, 1), :]
            m_prev = m_ref[...]
            m_new = jnp.maximum(m_prev, jnp.max(s, axis=0, keepdims=True) + bias)
            alpha = jnp.exp2(m_prev - m_new)
            p = jnp.exp2(s + (bias - m_new))
            l_ref[...] = alpha * l_ref[...] + jnp.sum(p, axis=0, keepdims=True)
            acc_ref[...] = alpha * acc_ref[...] + _dot(vt_ref[j], p.astype(BF16))
            m_ref[...] = m_new

        def two_blocks(t, carry):
            past_block(2 * t)
            past_block(2 * t + 1)
            return carry

        lax.fori_loop(0, qb // 2, two_blocks, 0)

        @pl.when(qb % 2 == 1)
        def _():
            past_block(qb - 1)

        outs.append(acc_ref[...] / l_ref[...])

    row = lax.broadcasted_iota(jnp.int32, (LANES, 1), 0)
    o_t = jnp.where(row < MOBA_HEAD_DIM, outs[0], outs[1])
    for g in range(MOBA_GROUP):
        gs = slice(g * LANES, (g + 1) * LANES)
        o = o_t[:, g * blk:(g + 1) * blk].T
        o_ref[:, gs] = (o * _silu(zc_ref[:, gs])).astype(BF16)


def _moba_prompt(h, n_seq, seq):
    nb = seq // MOBA_BLOCK
    cols = MOBA_GROUP * MOBA_BLOCK
    return pl.pallas_call(
        _moba_prompt_kernel,
        grid=(n_seq, nb),
        in_specs=[pl.BlockSpec((MOBA_BLOCK, W_C), lambda n, b: (n * nb + b, OFF_QC // W_C)),
                  pl.BlockSpec((MOBA_BLOCK, W_CKV), lambda n, b: (n * nb + b, OFF_KC // W_CKV)),
                  pl.BlockSpec((MOBA_BLOCK, W_CKV), lambda n, b: (n * nb + b, OFF_VC // W_CKV)),
                  pl.BlockSpec((MOBA_BLOCK, W_C), lambda n, b: (n * nb + b, OFF_ZC // W_C))],
        out_specs=pl.BlockSpec((MOBA_BLOCK, W_C), lambda n, b: (n * nb + b, 0)),
        out_shape=jax.ShapeDtypeStruct((n_seq * seq, W_C), BF16),
        scratch_shapes=[pltpu.VMEM((nb, MOBA_BLOCK, LANES), BF16),
                        pltpu.VMEM((nb, LANES, MOBA_BLOCK), BF16),
                        pltpu.VMEM((nb, LANES), F32),
                        pltpu.VMEM((cols, LANES), BF16),
                        pltpu.VMEM((nb, cols), F32),
                        pltpu.VMEM((1, cols), F32),
                        pltpu.VMEM((1, cols), F32),
                        pltpu.VMEM((LANES, cols), F32)],
        compiler_params=_params("parallel", "arbitrary"),
        name="moba_prompt",
    )(h, h, h, h)


def _moba_decode_one(q, kn, vn, k_refs, v_refs, s_ref, p_ref):
    n_pages = len(k_refs)
    pages_per_block = MOBA_BLOCK // PAGE_SIZE
    n_blocks = n_pages // pages_per_block
    lane = lax.broadcasted_iota(jnp.int32, (1, LANES), 1)

    gate = jnp.full((MOBA_HEADS, LANES), -jnp.inf, F32)
    for b in range(n_blocks):
        tot = jnp.zeros((MOBA_HEADS, PAGE_SIZE), F32)
        for i in range(pages_per_block):
            pg = b * pages_per_block + i
            s = _dot(q, k_refs[pg][...].astype(BF16))
            s_ref[pg] = s
            tot = tot + s
        gate = jnp.where(lane == b, jnp.sum(tot, axis=-1, keepdims=True), gate)
    picked = _topk_mask(gate, lane.astype(F32), MOBA_TOPK, axis=1)

    s_new = jnp.sum(q.astype(F32) * kn, axis=-1, keepdims=True)
    m_lanes = jnp.full((MOBA_HEADS, PAGE_SIZE), NEG, F32)
    for b in range(n_blocks):
        keep = picked[:, b:b + 1] > 0.0
        for i in range(pages_per_block):
            pg = b * pages_per_block + i
            s = jnp.where(keep, s_ref[pg], NEG)
            s_ref[pg] = s
            m_lanes = jnp.maximum(m_lanes, s)
    m = jnp.maximum(jnp.max(m_lanes, axis=-1, keepdims=True), s_new)
    p_new = jnp.exp2(s_new - m)
    l_lanes = jnp.zeros((MOBA_HEADS, PAGE_SIZE), F32)
    for pg in range(n_pages):
        p = jnp.exp2(s_ref[pg] - m)
        l_lanes = l_lanes + p
        p_ref[pg] = p.astype(BF16)
    l = p_new + jnp.sum(l_lanes, axis=-1, keepdims=True)
    acc = p_new * vn
    for pg in range(n_pages):
        acc = acc + _dot_nt(p_ref[pg], v_refs[pg][...].astype(BF16))
    o = acc / l
    return jnp.where(lane < MOBA_HEAD_DIM, o[:MOBA_GROUP], o[MOBA_GROUP:])


def _moba_decode_kernel(pt_ref, q_ref, kn_ref, vn_ref, k_hbm, v_hbm, o_ref, k_buf, v_buf, sems, s_ref, p_ref,
                        *, layer, n_pages):
    slot = _paged_prefetch(pt_ref, layer, n_pages, (k_hbm, v_hbm), (k_buf, v_buf), sems)
    for u in range(DECODE_SEQS):
        b = slot * DECODE_SEQS + u
        o_ref[u] = _moba_decode_one(q_ref[u], kn_ref[u], vn_ref[u],
                                    [k_buf.at[b, i] for i in range(n_pages)],
                                    [v_buf.at[b, i] for i in range(n_pages)], s_ref.at[u], p_ref.at[u])


def _moba_decode(page_table, q, knew, vnew, cache_k_t, cache_v_t, layer):
    b, n_pages = page_table.shape
    u = DECODE_SEQS
    grid_spec = pltpu.PrefetchScalarGridSpec(
        num_scalar_prefetch=1,
        grid=(b // u,),
        in_specs=[pl.BlockSpec((u, MOBA_HEADS, LANES), lambda s, pt: (s, 0, 0)),
                  pl.BlockSpec((u, 1, W_CKV), lambda s, pt: (s, 0, 0)),
                  pl.BlockSpec((u, 1, W_CKV), lambda s, pt: (s, 0, 0)),
                  pl.BlockSpec(memory_space=pl.ANY),
                  pl.BlockSpec(memory_space=pl.ANY)],
        out_specs=pl.BlockSpec((u, MOBA_GROUP, LANES), lambda s, pt: (s, 0, 0)),
        scratch_shapes=[pltpu.VMEM((2 * u, n_pages, W_CKV, PAGE_SIZE), F32),
                        pltpu.VMEM((2 * u, n_pages, W_CKV, PAGE_SIZE), F32),
                        pltpu.SemaphoreType.DMA((2, 2)),
                        pltpu.VMEM((u, n_pages, MOBA_HEADS, PAGE_SIZE), F32),
                        pltpu.VMEM((u, n_pages, MOBA_HEADS, PAGE_SIZE), BF16)])
    return pl.pallas_call(
        functools.partial(_moba_decode_kernel, layer=layer, n_pages=n_pages),
        grid_spec=grid_spec,
        out_shape=jax.ShapeDtypeStruct((b, MOBA_GROUP, LANES), F32),
        compiler_params=_params("arbitrary"),
        name="moba_decode",
    )(page_table, q, knew, vnew, cache_k_t, cache_v_t)


def _sample_out_kernel(ol_ref, oc_ref, zb_ref, zc_ref, wuv_ref, yb_ref, yc_ref):
    for h in range(MLA_HEADS):
        hs = slice(h * V_DIM, (h + 1) * V_DIM)
        y = _dot(ol_ref[h].astype(BF16), wuv_ref[h])
        yb_ref[:, hs] = (y * _silu(zb_ref[:, hs])).astype(BF16)
    yc_ref[...] = (oc_ref[...] * _silu(zc_ref[...])).astype(BF16)


def _sample_out(o_lat, o_c, h, wuv):
    b = h.shape[0]
    return pl.pallas_call(
        _sample_out_kernel,
        grid=(1,),
        in_specs=[pl.BlockSpec((MLA_HEADS, b, KV_LORA), lambda i: (0, 0, 0)),
                  pl.BlockSpec((b, W_C), lambda i: (0, 0)),
                  pl.BlockSpec((b, W_B), lambda i: (0, OFF_ZB // W_B)),
                  pl.BlockSpec((b, W_C), lambda i: (0, OFF_ZC // W_C)),
                  pl.BlockSpec((MLA_HEADS, KV_LORA, V_DIM), lambda i: (0, 0, 0))],
        out_specs=[pl.BlockSpec((b, W_B), lambda i: (0, 0)),
                   pl.BlockSpec((b, W_C), lambda i: (0, 0))],
        out_shape=[jax.ShapeDtypeStruct((b, W_B), BF16),
                   jax.ShapeDtypeStruct((b, W_C), BF16)],
        compiler_params=_params("arbitrary"),
        name="sample_out",
    )(o_lat, o_c, h, h, wuv)


def _merge_kernel(ya_ref, yb_ref, yc_ref, ga_ref, gb_ref, gc_ref, pa_ref, pb_ref, pc_ref, o_ref):
    m = (_sigmoid(ga_ref[...]) * _dot(ya_ref[...], pa_ref[...])
         + _sigmoid(gb_ref[...]) * _dot(yb_ref[...], pb_ref[...])
         + _sigmoid(gc_ref[...]) * _dot(yc_ref[...], pc_ref[...]))
    o_ref[...] = m.astype(BF16)


def _merge(ya, yb, yc, h, pa, pb, pc, tm, tn):
    r = ya.shape[0]
    gate_blk = lambda b: (lambda i, j: (i, (OFF_G + b * D_MODEL) // tn + j))
    row = lambda i, j: (i, 0)
    col = lambda i, j: (0, j)
    return pl.pallas_call(
        _merge_kernel,
        grid=(r // tm, D_MODEL // tn),
        in_specs=[pl.BlockSpec((tm, W_A), row), pl.BlockSpec((tm, W_B), row), pl.BlockSpec((tm, W_C), row),
                  pl.BlockSpec((tm, tn), gate_blk(0)), pl.BlockSpec((tm, tn), gate_blk(1)),
                  pl.BlockSpec((tm, tn), gate_blk(2)),
                  pl.BlockSpec((W_A, tn), col), pl.BlockSpec((W_B, tn), col), pl.BlockSpec((W_C, tn), col)],
        out_specs=pl.BlockSpec((tm, tn), lambda i, j: (i, j)),
        out_shape=jax.ShapeDtypeStruct((r, D_MODEL), BF16),
        compiler_params=_params("parallel", "arbitrary"),
        name="merge",
    )(ya, yb, yc, h, h, h, pa, pb, pc)


def _outproj_kernel(m_ref, w_ref, x_ref, o_ref):
    o_ref[...] = x_ref[...] + _dot(m_ref[...], w_ref[...])


def _outproj(m, w, x, tm, tn):
    r = m.shape[0]
    return pl.pallas_call(
        _outproj_kernel,
        grid=(r // tm, D_MODEL // tn),
        in_specs=[pl.BlockSpec((tm, D_MODEL), lambda i, j: (i, 0)),
                  pl.BlockSpec((D_MODEL, tn), lambda i, j: (0, j)),
                  pl.BlockSpec((tm, tn), lambda i, j: (i, j))],
        out_specs=pl.BlockSpec((tm, tn), lambda i, j: (i, j)),
        out_shape=jax.ShapeDtypeStruct((r, D_MODEL), F32),
        compiler_params=_params("parallel", "arbitrary"),
        name="outproj",
    )(m, w, x)


def _final_norm_kernel(x_ref, g_ref, o_ref):
    x = x_ref[...]
    ms = jnp.mean(x * x, axis=-1, keepdims=True)
    o_ref[...] = x * lax.rsqrt(ms + EPS) * g_ref[...]


def _final_norm(x, g, tm):
    r = x.shape[0]
    return pl.pallas_call(
        _final_norm_kernel,
        grid=(r // tm,),
        in_specs=[pl.BlockSpec((tm, D_MODEL), lambda i: (i, 0)),
                  pl.BlockSpec((1, D_MODEL), lambda i: (0, 0))],
        out_specs=pl.BlockSpec((tm, D_MODEL), lambda i: (i, 0)),
        out_shape=jax.ShapeDtypeStruct((r, D_MODEL), F32),
        compiler_params=_params("parallel"),
        name="final_norm",
    )(x, g)


def _pair_heads(w):
    lead = w.shape[:-1]
    w = w.reshape(lead + (MOBA_KV_HEADS, MOBA_GROUP, MOBA_HEAD_DIM))
    return jnp.swapaxes(w, -3, -2).reshape(lead + (W_C,))


def _swap_halves(w):
    half = ROPE_DIM // 2
    return jnp.concatenate([w[..., half:], w[..., :half]], axis=-1)


def _prep_weights(w_in, w_pool, w_uq, w_uk, w_uv, p_a, p_b, p_c, w_out):
    offs = np.cumsum((0,) + IN_SPLITS)
    w_in_t = jnp.swapaxes(w_in, 1, 2)
    piece = lambda i: w_in_t[:, offs[i]:offs[i + 1], :]
    u_a, z_a, c_q, c_kv, k_pe, z_b, q_c, k_c, v_c, z_c, g = [piece(i) for i in range(len(IN_SPLITS))]
    pair_rows = lambda t: jnp.swapaxes(_pair_heads(jnp.swapaxes(t, 1, 2)), 1, 2)
    swap_rows = lambda t: jnp.swapaxes(_swap_halves(jnp.swapaxes(t, 1, 2)), 1, 2)
    w_in_p = jnp.concatenate(
        [z_b, u_a, z_a, pair_rows(q_c), pair_rows(z_c), g, c_q, c_kv, k_c, v_c, k_pe, swap_rows(k_pe)],
        axis=1).astype(BF16)
    wq = w_uq.reshape(DEPTH, Q_LORA, MLA_HEADS, NOPE_DIM + ROPE_DIM)
    wqn = wq[..., :NOPE_DIM].reshape(DEPTH, Q_LORA, MLA_HEADS * NOPE_DIM).astype(BF16)
    pe = wq[..., NOPE_DIM:]
    wqp = jnp.concatenate([pe, _swap_halves(pe)], axis=-1).reshape(DEPTH, Q_LORA, MLA_HEADS * LANES).astype(BF16)
    wuk = jnp.transpose(w_uk, (0, 2, 3, 1)).astype(BF16)
    wuv = jnp.transpose(w_uv, (0, 2, 1, 3)).astype(BF16)
    p_c_p = jnp.swapaxes(p_c.reshape(DEPTH, MOBA_KV_HEADS, MOBA_GROUP, MOBA_HEAD_DIM, D_MODEL), 1, 2)
    p_c_p = p_c_p.reshape(DEPTH, W_C, D_MODEL).astype(BF16)
    return dict(w_in=w_in_p, w_pool=w_pool.astype(BF16), wqn=wqn, wqp=wqp, wuk=wuk, wuv=wuv,
                p_a=p_a.astype(BF16), p_b=p_b.astype(BF16), p_c=p_c_p, w_out=w_out.astype(BF16))


def _rope_table(pos):
    half = ROPE_DIM // 2
    inv = ROPE_THETA ** (-jnp.arange(half, dtype=F32) / half)
    ang = pos.astype(F32)[:, None] * inv[None, :]
    cos, sin = jnp.cos(ang), jnp.sin(ang)
    return jnp.concatenate([cos, cos, -sin, sin], axis=-1)


def _layer_prompt(x, w, l, cs, norm_g, pool_scale, q_norm_g, kv_norm_g, n_seq, seq):
    h = _inproj(x, norm_g[l][None], w["w_in"][l], tm=1024, tn=896)
    ya = _pool_prompt(h, w["w_pool"][l], pool_scale[l][None], n_seq, seq)
    q, kcat, ckv_t, ckv_new, kpe_new = _mla_prep(h, cs, q_norm_g[l][None], kv_norm_g[l][None],
                                                 w["wqn"][l], w["wqp"][l], w["wuk"][l], tm=256)
    yb = _mla_prompt(q, kcat, ckv_t, h, w["wuv"][l], n_seq, seq)
    yc = _moba_prompt(h, n_seq, seq)
    m = _merge(ya, yb, yc, h, w["p_a"][l], w["p_b"][l], w["p_c"][l], tm=1024, tn=512)
    x = _outproj(m, w["w_out"][l], x, tm=1024, tn=512)
    state = (h[:, OFF_UA:OFF_UA + W_A].reshape(n_seq, seq, W_A)[:, seq - POOL_HIST:],
             ckv_new.reshape(n_seq, seq, KV_LORA),
             kpe_new.reshape(n_seq, seq, ROPE_DIM),
             h[:, OFF_KC:OFF_KC + W_CKV].reshape(n_seq, seq, MOBA_KV_HEADS, MOBA_HEAD_DIM),
             h[:, OFF_VC:OFF_VC + W_CKV].reshape(n_seq, seq, MOBA_KV_HEADS, MOBA_HEAD_DIM))
    return x, state


def _layer_sample(x, w, l, cs, norm_g, pool_scale, q_norm_g, kv_norm_g,
                  hist2d, cache_ckv, cache_kpe_t, cache_k_t, cache_v_t, page_table):
    b = x.shape[0]
    h = _inproj(x, norm_g[l][None], w["w_in"][l], tm=b, tn=1920)
    ya, pool_new = _pool_sample(hist2d, h, w["w_pool"][l], pool_scale[l][None])
    q, kcat, _, ckv_new, kpe_new = _mla_prep(h, cs, q_norm_g[l][None], kv_norm_g[l][None],
                                             w["wqn"][l], w["wqp"][l], w["wuk"][l], tm=b)
    o_lat = _mla_decode(page_table, jnp.swapaxes(q, 0, 1), kcat[:, None, :], cache_ckv, cache_kpe_t, l)
    qc = h[:, OFF_QC:OFF_QC + W_C].reshape(b, 1, MOBA_GROUP, LANES) * MOBA_SCALE
    half = (jnp.arange(LANES)[None, None, None, :] // MOBA_HEAD_DIM
            == jnp.arange(MOBA_KV_HEADS)[None, :, None, None])
    qd = jnp.where(half, qc, 0.0).reshape(b, MOBA_HEADS, LANES).astype(BF16)
    k_new = h[:, OFF_KC:OFF_KC + W_CKV]
    v_new = h[:, OFF_VC:OFF_VC + W_CKV]
    o_c = _moba_decode(page_table, qd, k_new[:, None, :], v_new[:, None, :], cache_k_t, cache_v_t, l)
    yb, yc = _sample_out(jnp.swapaxes(o_lat, 0, 1), o_c.reshape(b, W_C), h, w["wuv"][l])
    m = _merge(ya, yb, yc, h, w["p_a"][l], w["p_b"][l], w["p_c"][l], tm=b, tn=512)
    x = _outproj(m, w["w_out"][l], x, tm=b, tn=512)
    state = (pool_new.reshape(b, POOL_HIST, W_A),
             ckv_new.reshape(b, 1, KV_LORA),
             kpe_new.reshape(b, 1, ROPE_DIM),
             k_new.reshape(b, 1, MOBA_KV_HEADS, MOBA_HEAD_DIM),
             v_new.reshape(b, 1, MOBA_KV_HEADS, MOBA_HEAD_DIM))
    return x, state


@jax.jit
def _forward(x_prompt, x_sample, state_pool, cache_ckv, cache_kpe, cache_k, cache_v, page_table,
             norm_g, w_in, w_pool, pool_scale, q_norm_g, kv_norm_g, w_uq, w_uk, w_uv,
             p_a, p_b, p_c, w_out, final_g):
    n_seq, seq, _ = x_prompt.shape
    b = x_sample.shape[0]
    n_pool = cache_k.shape[1]
    past_len = page_table.shape[1] * PAGE_SIZE
    w = _prep_weights(w_in, w_pool, w_uq, w_uk, w_uv, p_a, p_b, p_c, w_out)
    cs_p = jnp.tile(_rope_table(jnp.arange(seq)), (n_seq, 1))
    cs_s = jnp.tile(_rope_table(jnp.full((1,), past_len)), (b, 1))
    cache_kpe_t = jnp.swapaxes(cache_kpe, 2, 3)
    cache_k_t = jnp.transpose(cache_k, (0, 1, 3, 4, 2)).reshape(DEPTH, n_pool, W_CKV, PAGE_SIZE)
    cache_v_t = jnp.transpose(cache_v, (0, 1, 3, 4, 2)).reshape(DEPTH, n_pool, W_CKV, PAGE_SIZE)
    hist2d = state_pool.reshape(DEPTH, b, POOL_HIST * W_A)

    xp = x_prompt.reshape(n_seq * seq, D_MODEL)
    xs = x_sample.reshape(b, D_MODEL)
    new_p, new_s = [], []
    for l in range(DEPTH):
        xs, st_s = _layer_sample(xs, w, l, cs_s, norm_g, pool_scale, q_norm_g, kv_norm_g,
                                 hist2d[l], cache_ckv, cache_kpe_t, cache_k_t, cache_v_t, page_table)
        xp, st_p = _layer_prompt(xp, w, l, cs_p, norm_g, pool_scale, q_norm_g, kv_norm_g, n_seq, seq)
        new_p.append(st_p)
        new_s.append(st_s)
    y_prompt = _final_norm(xp, final_g[None], tm=1024).reshape(n_seq, seq, D_MODEL)
    y_sample = _final_norm(xs, final_g[None], tm=b).reshape(b, 1, D_MODEL)
    stack = lambda sts, i: jnp.stack([st[i] for st in sts])
    return ((y_prompt, y_sample)
            + tuple(stack(new_p, i) for i in range(5))
            + tuple(stack(new_s, i) for i in range(5)))


def kernel(x_prompt, x_sample, state_pool, cache_ckv, cache_kpe, cache_k, cache_v, page_table,
           norm_g, w_in, w_pool, pool_scale, q_norm_g, kv_norm_g, w_uq, w_uk, w_uv,
           p_a, p_b, p_c, w_out, final_g):
    return _forward(x_prompt, x_sample, state_pool, cache_ckv, cache_kpe, cache_k, cache_v, page_table,
                    norm_g, w_in, w_pool, pool_scale, q_norm_g, kv_norm_g, w_uq, w_uk, w_uv,
                    p_a, p_b, p_c, w_out, final_g)
```

```python
import functools

import jax
import jax.numpy as jnp
import numpy as np
from jax import lax
from jax.experimental import pallas as pl
from jax.experimental.pallas import tpu as pltpu

F32 = jnp.float32
BF16 = jnp.bfloat16

D_MODEL = 2048
DEPTH = 4
PAGE_SIZE = 128
POOL_WINDOWS = (2, 4, 8, 16)
POOL_GROUP = 256
W_A = len(POOL_WINDOWS) * POOL_GROUP
POOL_HIST = max(POOL_WINDOWS) - 1
MLA_HEADS = 16
Q_LORA = 512
KV_LORA = 256
NOPE_DIM = 128
ROPE_DIM = 64
V_DIM = 128
W_B = MLA_HEADS * V_DIM
ROPE_THETA = 10000.0
MOBA_HEADS = 16
MOBA_KV_HEADS = 2
MOBA_GROUP = MOBA_HEADS // MOBA_KV_HEADS
MOBA_HEAD_DIM = 64
MOBA_BLOCK = 256
MOBA_TOPK = 3
W_C = MOBA_HEADS * MOBA_HEAD_DIM
W_CKV = MOBA_KV_HEADS * MOBA_HEAD_DIM
N_BRANCH = 3
EPS = 1e-6
NEG = -1e30
IN_SPLITS = (W_A, W_A, Q_LORA, KV_LORA, ROPE_DIM, W_B, W_C, W_CKV, W_CKV, W_C, N_BRANCH * D_MODEL)

LOG2_E = 1.4426950408889634
MLA_SCALE = (NOPE_DIM + ROPE_DIM) ** -0.5 * LOG2_E
MOBA_SCALE = MOBA_HEAD_DIM ** -0.5 * LOG2_E
LANES = 128
QK_WIDTH = KV_LORA + LANES
DECODE_SEQS = 2

OFF_ZB = 0
OFF_ZA = OFF_ZB + W_B
OFF_QC = OFF_ZA + W_A
OFF_ZC = OFF_QC + W_C
OFF_G = OFF_ZC + W_C
N_H16 = OFF_G + N_BRANCH * D_MODEL
OFF_UA = 0
OFF_CQ = OFF_UA + W_A
OFF_CKV = OFF_CQ + Q_LORA
OFF_KC = OFF_CKV + KV_LORA
OFF_VC = OFF_KC + W_CKV
OFF_KPE = OFF_VC + W_CKV
N_H32 = OFF_KPE + 2 * ROPE_DIM

VMEM_LIMIT = 56 * 1024 * 1024


def _params(*sem):
    return pltpu.CompilerParams(dimension_semantics=sem, vmem_limit_bytes=VMEM_LIMIT)


def _silu(z):
    z = z.astype(F32)
    return z * (1.0 / (1.0 + jnp.exp(-z)))


def _sigmoid(z):
    z = z.astype(F32)
    return 1.0 / (1.0 + jnp.exp(-z))


def _dot(a, b):
    return jnp.dot(a, b, preferred_element_type=F32)


def _dot_nt(a, b):
    return lax.dot_general(a, b, (((1,), (1,)), ((), ())), preferred_element_type=F32)


def _inproj_kernel(x_ref, g_ref, w_ref, o_ref, xn_ref):
    @pl.when(pl.program_id(1) == 0)
    def _():
        x = x_ref[...]
        ms = jnp.mean(x * x, axis=-1, keepdims=True)
        xn_ref[...] = (x * lax.rsqrt(ms + EPS) * g_ref[...]).astype(BF16)

    o_ref[...] = _dot_nt(xn_ref[...], w_ref[...]).astype(o_ref.dtype)


def _inproj(x, g, w, tm, tn, out_dtype):
    r, n = x.shape[0], w.shape[0]
    return pl.pallas_call(
        _inproj_kernel,
        grid=(r // tm, n // tn),
        in_specs=[pl.BlockSpec((tm, D_MODEL), lambda i, j: (i, 0)),
                  pl.BlockSpec((1, D_MODEL), lambda i, j: (0, 0)),
                  pl.BlockSpec((tn, D_MODEL), lambda i, j: (j, 0))],
        out_specs=pl.BlockSpec((tm, tn), lambda i, j: (i, j)),
        out_shape=jax.ShapeDtypeStruct((r, n), out_dtype),
        scratch_shapes=[pltpu.VMEM((tm, D_MODEL), BF16)],
        compiler_params=_params("parallel", "arbitrary"),
        name="inproj",
    )(x, g, w)


def _pool_finish(win_sum, cnt, u, z, wp_ref, sc_ref, g):
    cols = slice(g * POOL_GROUP, (g + 1) * POOL_GROUP)
    d = (win_sum / cnt - u).astype(BF16)
    y = _dot(d, wp_ref[g]) * sc_ref[:, cols]
    return (y * _silu(z)).astype(BF16)


def _pool_prompt_kernel(u_ref, z_ref, wp_ref, sc_ref, o_ref, ext_ref, *, ts):
    t = pl.program_id(1)
    halo = POOL_HIST + 1

    @pl.when(t == 0)
    def _():
        ext_ref[0:halo, :] = jnp.zeros((halo, W_A), F32)

    @pl.when(t > 0)
    def _():
        ext_ref[0:halo, :] = ext_ref[ts:ts + halo, :]

    ext_ref[halo:, :] = u_ref[...]
    pos = t * ts + lax.broadcasted_iota(jnp.int32, (ts, 1), 0)
    for g, w in enumerate(POOL_WINDOWS):
        cols = slice(g * POOL_GROUP, (g + 1) * POOL_GROUP)
        u = u_ref[:, cols]
        win_sum = u
        for k in range(1, w):
            win_sum = win_sum + ext_ref[halo - k:halo - k + ts, cols]
        cnt = jnp.minimum(w, pos + 1).astype(F32)
        o_ref[:, cols] = _pool_finish(win_sum, cnt, u, z_ref[:, cols], wp_ref, sc_ref, g)


def _pool_prompt(h, hn, wp, sc, n_seq, seq, ts=512):
    nt = seq // ts
    blk = W_A
    return pl.pallas_call(
        functools.partial(_pool_prompt_kernel, ts=ts),
        grid=(n_seq, nt),
        in_specs=[pl.BlockSpec((ts, blk), lambda n, t: (n * nt + t, OFF_UA // blk)),
                  pl.BlockSpec((ts, blk), lambda n, t: (n * nt + t, OFF_ZA // blk)),
                  pl.BlockSpec((len(POOL_WINDOWS), POOL_GROUP, POOL_GROUP), lambda n, t: (0, 0, 0)),
                  pl.BlockSpec((1, W_A), lambda n, t: (0, 0))],
        out_specs=pl.BlockSpec((ts, blk), lambda n, t: (n * nt + t, 0)),
        out_shape=jax.ShapeDtypeStruct((n_seq * seq, W_A), BF16),
        scratch_shapes=[pltpu.VMEM((POOL_HIST + 1 + ts, W_A), F32)],
        compiler_params=_params("parallel", "arbitrary"),
        name="pool_prompt",
    )(h, hn, wp, sc)


def _pool_sample_kernel(hist_ref, u_ref, z_ref, wp_ref, sc_ref, o_ref, st_ref):
    for g, w in enumerate(POOL_WINDOWS):
        cols = slice(g * POOL_GROUP, (g + 1) * POOL_GROUP)
        u = u_ref[:, cols]
        win_sum = u
        for k in range(1, w):
            off = (POOL_HIST - k) * W_A + g * POOL_GROUP
            win_sum = win_sum + hist_ref[:, off:off + POOL_GROUP]
        o_ref[:, cols] = _pool_finish(win_sum, float(w), u, z_ref[:, cols], wp_ref, sc_ref, g)
    st_ref[:, :(POOL_HIST - 1) * W_A] = hist_ref[:, W_A:]
    st_ref[:, (POOL_HIST - 1) * W_A:] = u_ref[...]


def _pool_sample(hist2d, h, hn, wp, sc):
    b = h.shape[0]
    blk = W_A
    return pl.pallas_call(
        _pool_sample_kernel,
        grid=(1,),
        in_specs=[pl.BlockSpec((b, POOL_HIST * W_A), lambda i: (0, 0)),
                  pl.BlockSpec((b, blk), lambda i: (0, OFF_UA // blk)),
                  pl.BlockSpec((b, blk), lambda i: (0, OFF_ZA // blk)),
                  pl.BlockSpec((len(POOL_WINDOWS), POOL_GROUP, POOL_GROUP), lambda i: (0, 0, 0)),
                  pl.BlockSpec((1, W_A), lambda i: (0, 0))],
        out_specs=[pl.BlockSpec((b, W_A), lambda i: (0, 0)),
                   pl.BlockSpec((b, POOL_HIST * W_A), lambda i: (0, 0))],
        out_shape=[jax.ShapeDtypeStruct((b, W_A), BF16),
                   jax.ShapeDtypeStruct((b, POOL_HIST * W_A), F32)],
        compiler_params=_params("arbitrary"),
        name="pool_sample",
    )(hist2d, h, hn, wp, sc)


def _mla_prep_kernel(cq_ref, ckv_ref, kpe_ref, cs_ref, gq_ref, gkv_ref, wqn_ref, wqp_ref, wuk_ref,
                     q_ref, kcat_ref, ckvt_ref, ckvn_ref, kper_ref):
    lane = lax.broadcasted_iota(jnp.int32, (1, LANES), 1)
    first_half = lane < ROPE_DIM
    cs = cs_ref[...]

    cq = cq_ref[...]
    ms = jnp.mean(cq * cq, axis=-1, keepdims=True)
    cqn = (cq * lax.rsqrt(ms + EPS) * gq_ref[...]).astype(BF16)
    qn = _dot(cqn, wqn_ref[...])
    qp = _dot(cqn, wqp_ref[...])
    t = qp * jnp.concatenate([cs] * MLA_HEADS, axis=1)
    rot = t + pltpu.roll(t, MLA_HEADS * LANES - ROPE_DIM, axis=1)
    for h in range(MLA_HEADS):
        hs = slice(h * LANES, (h + 1) * LANES)
        ql = _dot(qn[:, hs].astype(BF16), wuk_ref[h]) * MLA_SCALE
        q_ref[h, :, :KV_LORA] = ql.astype(BF16)
        q_ref[h, :, KV_LORA:] = jnp.where(first_half, rot[:, hs] * MLA_SCALE, 0.0).astype(BF16)

    ckv = ckv_ref[...]
    ms = jnp.mean(ckv * ckv, axis=-1, keepdims=True)
    ckvn = ckv * lax.rsqrt(ms + EPS) * gkv_ref[...]
    ckvn_ref[...] = ckvn
    kcat_ref[:, :KV_LORA] = ckvn.astype(BF16)
    ckvt_ref[...] = ckvn.T.astype(BF16)
    tk = kpe_ref[...] * cs
    krot = tk + pltpu.roll(tk, ROPE_DIM, axis=1)
    kper_ref[...] = krot[:, :ROPE_DIM]
    kcat_ref[:, KV_LORA:] = jnp.where(first_half, krot, 0.0).astype(BF16)


def _mla_prep(h, cs, gq, gkv, wqn, wqp, wuk, tm):
    r = h.shape[0]
    const2 = lambda i: (0, 0)
    return pl.pallas_call(
        _mla_prep_kernel,
        grid=(r // tm,),
        in_specs=[pl.BlockSpec((tm, Q_LORA), lambda i: (i, OFF_CQ // Q_LORA)),
                  pl.BlockSpec((tm, KV_LORA), lambda i: (i, OFF_CKV // KV_LORA)),
                  pl.BlockSpec((tm, LANES), lambda i: (i, OFF_KPE // LANES)),
                  pl.BlockSpec((tm, LANES), lambda i: (i, 0)),
                  pl.BlockSpec((1, Q_LORA), const2),
                  pl.BlockSpec((1, KV_LORA), const2),
                  pl.BlockSpec((Q_LORA, MLA_HEADS * NOPE_DIM), const2),
                  pl.BlockSpec((Q_LORA, MLA_HEADS * LANES), const2),
                  pl.BlockSpec((MLA_HEADS, NOPE_DIM, KV_LORA), lambda i: (0, 0, 0))],
        out_specs=[pl.BlockSpec((MLA_HEADS, tm, QK_WIDTH), lambda i: (0, i, 0)),
                   pl.BlockSpec((tm, QK_WIDTH), lambda i: (i, 0)),
                   pl.BlockSpec((KV_LORA, tm), lambda i: (0, i)),
                   pl.BlockSpec((tm, KV_LORA), lambda i: (i, 0)),
                   pl.BlockSpec((tm, ROPE_DIM), lambda i: (i, 0))],
        out_shape=[jax.ShapeDtypeStruct((MLA_HEADS, r, QK_WIDTH), BF16),
                   jax.ShapeDtypeStruct((r, QK_WIDTH), BF16),
                   jax.ShapeDtypeStruct((KV_LORA, r), BF16),
                   jax.ShapeDtypeStruct((r, KV_LORA), F32),
                   jax.ShapeDtypeStruct((r, ROPE_DIM), F32)],
        compiler_params=_params("parallel"),
        name="mla_prep",
    )(h, h, h, cs, gq, gkv, wqn, wqp, wuk)


def _mla_prompt_kernel(qi_ref, ki_ref, q_ref, k_ref, kt_ref, zb_ref, wuv_ref, o_ref, m_ref, l_ref, acc_ref,
                       *, tq, tk):
    qi = qi_ref[pl.program_id(1)]
    ki = ki_ref[pl.program_id(1)]
    last = (qi * tq + tq - 1) // tk
    cols = MLA_HEADS * tq

    @pl.when(ki == 0)
    def _():
        m_ref[...] = jnp.full((1, cols), NEG, F32)
        l_ref[...] = jnp.zeros((1, cols), F32)
        acc_ref[...] = jnp.zeros((KV_LORA, cols), F32)

    def update(masked):
        q = q_ref[...].reshape(cols, QK_WIDTH)
        s = _dot_nt(k_ref[...], q)
        if masked:
            qpos = qi * tq + (lax.broadcasted_iota(jnp.int32, (1, cols), 1) & (tq - 1))
            kpos = ki * tk + lax.broadcasted_iota(jnp.int32, (tk, 1), 0)
            s = jnp.where(kpos <= qpos, s, NEG)
        m_prev = m_ref[...]
        m_new = jnp.maximum(m_prev, jnp.max(s, axis=0, keepdims=True))
        alpha = jnp.exp2(m_prev - m_new)
        p = jnp.exp2(s - m_new)
        l_ref[...] = alpha * l_ref[...] + jnp.sum(p, axis=0, keepdims=True)
        acc_ref[...] = alpha * acc_ref[...] + _dot(kt_ref[...], p.astype(BF16))
        m_ref[...] = m_new

    @pl.when(ki < last)
    def _():
        update(False)

    @pl.when(ki == last)
    def _():
        update(True)
        o_t = acc_ref[...] / l_ref[...]
        for h in range(MLA_HEADS):
            hs = slice(h * V_DIM, (h + 1) * V_DIM)
            o_h = o_t[:, h * tq:(h + 1) * tq].T.astype(BF16)
            y = _dot(o_h, wuv_ref[h])
            o_ref[:, hs] = (y * _silu(zb_ref[:, hs])).astype(BF16)


def _mla_prompt(q, kcat, ckv_t, h, wuv, n_seq, seq, tq=256, tk=512):
    nq, nk = seq // tq, seq // tk
    pairs = [(i, j) for i in range(nq) for j in range((i * tq + tq - 1) // tk + 1)]
    qi_tab = jnp.asarray([p[0] for p in pairs], jnp.int32)
    ki_tab = jnp.asarray([p[1] for p in pairs], jnp.int32)
    grid_spec = pltpu.PrefetchScalarGridSpec(
        num_scalar_prefetch=2,
        grid=(n_seq, len(pairs)),
        in_specs=[pl.BlockSpec((MLA_HEADS, tq, QK_WIDTH), lambda n, t, qi, ki: (0, n * nq + qi[t], 0)),
                  pl.BlockSpec((tk, QK_WIDTH), lambda n, t, qi, ki: (n * nk + ki[t], 0)),
                  pl.BlockSpec((KV_LORA, tk), lambda n, t, qi, ki: (0, n * nk + ki[t])),
                  pl.BlockSpec((tq, W_B), lambda n, t, qi, ki: (n * nq + qi[t], OFF_ZB // W_B)),
                  pl.BlockSpec((MLA_HEADS, KV_LORA, V_DIM), lambda n, t, qi, ki: (0, 0, 0))],
        out_specs=pl.BlockSpec((tq, W_B), lambda n, t, qi, ki: (n * nq + qi[t], 0)),
        scratch_shapes=[pltpu.VMEM((1, MLA_HEADS * tq), F32),
                        pltpu.VMEM((1, MLA_HEADS * tq), F32),
                        pltpu.VMEM((KV_LORA, MLA_HEADS * tq), F32)])
    return pl.pallas_call(
        functools.partial(_mla_prompt_kernel, tq=tq, tk=tk),
        grid_spec=grid_spec,
        out_shape=jax.ShapeDtypeStruct((n_seq * seq, W_B), BF16),
        compiler_params=_params("parallel", "arbitrary"),
        name="mla_prompt",
    )(qi_tab, ki_tab, q, kcat, ckv_t, h, wuv)


def _page_copies(pt_ref, step, slot, layer, n_pages, caches, bufs, sems):
    copies = []
    for u in range(DECODE_SEQS):
        for i in range(n_pages):
            page = pt_ref[step * DECODE_SEQS + u, i]
            for a, (cache, buf) in enumerate(zip(caches, bufs)):
                copies.append(pltpu.make_async_copy(
                    cache.at[layer, page], buf.at[slot * DECODE_SEQS + u, i], sems.at[a, slot]))
    return copies


def _paged_prefetch(pt_ref, layer, n_pages, caches, bufs, sems):
    step = pl.program_id(0)
    slot = step % 2

    @pl.when(step == 0)
    def _():
        for c in _page_copies(pt_ref, step, slot, layer, n_pages, caches, bufs, sems):
            c.start()

    @pl.when(step + 1 < pl.num_programs(0))
    def _():
        for c in _page_copies(pt_ref, step + 1, 1 - slot, layer, n_pages, caches, bufs, sems):
            c.start()

    for c in _page_copies(pt_ref, step, slot, layer, n_pages, caches, bufs, sems):
        c.wait()
    return slot


def _mla_decode_one(q, kn, ckv_refs, kpe_refs, s_ref, p_ref):
    n_pages = len(ckv_refs)
    ql = q[:, :KV_LORA]
    qp = q[:, KV_LORA:KV_LORA + ROPE_DIM]
    s_new = jnp.sum(q.astype(F32) * kn, axis=-1, keepdims=True)
    m_lanes = jnp.full((MLA_HEADS, PAGE_SIZE), NEG, F32)
    for i in range(n_pages):
        s = _dot_nt(ql, ckv_refs[i][...].astype(BF16)) + _dot(qp, kpe_refs[i][...].astype(BF16))
        s_ref[i] = s
        m_lanes = jnp.maximum(m_lanes, s)
    m = jnp.maximum(jnp.max(m_lanes, axis=-1, keepdims=True), s_new)
    p_new = jnp.exp2(s_new - m)
    l_lanes = jnp.zeros((MLA_HEADS, PAGE_SIZE), F32)
    for i in range(n_pages):
        p = jnp.exp2(s_ref[i] - m)
        l_lanes = l_lanes + p
        p_ref[i] = p.astype(BF16)
    l = p_new + jnp.sum(l_lanes, axis=-1, keepdims=True)
    acc = p_new * kn[:, :KV_LORA]
    for i in range(n_pages):
        acc = acc + _dot(p_ref[i], ckv_refs[i][...].astype(BF16))
    return acc / l


def _mla_decode_kernel(pt_ref, q_ref, kn_ref, ckv_hbm, kpe_hbm, o_ref, ckv_buf, kpe_buf, sems, s_ref, p_ref,
                       *, layer, n_pages):
    slot = _paged_prefetch(pt_ref, layer, n_pages, (ckv_hbm, kpe_hbm), (ckv_buf, kpe_buf), sems)
    for u in range(DECODE_SEQS):
        b = slot * DECODE_SEQS + u
        o_ref[u] = _mla_decode_one(q_ref[u], kn_ref[u].astype(F32),
                                   [ckv_buf.at[b, i] for i in range(n_pages)],
                                   [kpe_buf.at[b, i] for i in range(n_pages)], s_ref.at[u], p_ref.at[u])


def _mla_decode(page_table, q, knew, cache_ckv, cache_kpe_t, layer):
    b, n_pages = page_table.shape
    u = DECODE_SEQS
    grid_spec = pltpu.PrefetchScalarGridSpec(
        num_scalar_prefetch=1,
        grid=(b // u,),
        in_specs=[pl.BlockSpec((u, MLA_HEADS, QK_WIDTH), lambda s, pt: (s, 0, 0)),
                  pl.BlockSpec((u, 1, QK_WIDTH), lambda s, pt: (s, 0, 0)),
                  pl.BlockSpec(memory_space=pl.ANY),
                  pl.BlockSpec(memory_space=pl.ANY)],
        out_specs=pl.BlockSpec((u, MLA_HEADS, KV_LORA), lambda s, pt: (s, 0, 0)),
        scratch_shapes=[pltpu.VMEM((2 * u, n_pages, PAGE_SIZE, KV_LORA), F32),
                        pltpu.VMEM((2 * u, n_pages, ROPE_DIM, PAGE_SIZE), F32),
                        pltpu.SemaphoreType.DMA((2, 2)),
                        pltpu.VMEM((u, n_pages, MLA_HEADS, PAGE_SIZE), F32),
                        pltpu.VMEM((u, n_pages, MLA_HEADS, PAGE_SIZE), BF16)])
    return pl.pallas_call(
        functools.partial(_mla_decode_kernel, layer=layer, n_pages=n_pages),
        grid_spec=grid_spec,
        out_shape=jax.ShapeDtypeStruct((b, MLA_HEADS, KV_LORA), F32),
        compiler_params=_params("arbitrary"),
        name="mla_decode",
    )(page_table, q, knew, cache_ckv, cache_kpe_t)


def _topk_mask(gate, idx_f, n_keep, axis):
    picked = jnp.zeros(gate.shape, F32)
    for r in range(MOBA_TOPK):
        mx = jnp.max(gate, axis=axis, keepdims=True)
        idx = jnp.min(jnp.where(gate == mx, idx_f, 1e9), axis=axis, keepdims=True)
        hit = idx_f == idx
        keep = jnp.where(r < n_keep, 1.0, 0.0)
        picked = jnp.maximum(picked, jnp.where(hit, keep, 0.0))
        gate = jnp.where(hit, -jnp.inf, gate)
    return picked


def _moba_prompt_kernel(q_ref, k_ref, v_ref, zc_ref, o_ref,
                        kb_ref, vt_ref, km_ref, qg_ref, bias_ref, m_ref, l_ref, acc_ref):
    qb = pl.program_id(1)
    blk = MOBA_BLOCK
    cols = MOBA_GROUP * blk
    n_blocks = km_ref.shape[0]
    lane = lax.broadcasted_iota(jnp.int32, (1, LANES), 1)

    @pl.when(qb == 0)
    def _():
        km_ref[...] = jnp.zeros(km_ref.shape, F32)

    k_own = k_ref[...]
    kb = k_own.astype(BF16)
    vt = v_ref[...].T.astype(BF16)
    kb_ref[qb] = kb
    vt_ref[qb] = vt
    kmean = km_ref[...].astype(BF16)
    km_ref[pl.ds(qb, 1), :] = jnp.sum(k_own, axis=0, keepdims=True) * (1.0 / blk)

    qf = q_ref[...].astype(F32)
    blk_idx = lax.broadcasted_iota(jnp.int32, (n_blocks, 1), 0)
    key_idx = lax.broadcasted_iota(jnp.int32, (blk, 1), 0)
    q_idx = lax.broadcasted_iota(jnp.int32, (1, cols), 1) & (blk - 1)
    causal = key_idx <= q_idx

    outs = []
    for c in range(MOBA_KV_HEADS):
        mine = (lane >= c * MOBA_HEAD_DIM) & (lane < (c + 1) * MOBA_HEAD_DIM)
        qg_ref[...] = (jnp.concatenate(
            [jnp.where(mine, qf[:, g * LANES:(g + 1) * LANES], 0.0) for g in range(MOBA_GROUP)],
            axis=0) * MOBA_SCALE).astype(BF16)
        gate = _dot_nt(kmean, qg_ref[...])
        picked = _topk_mask(jnp.where(blk_idx < qb, gate, -jnp.inf), blk_idx.astype(F32), qb, axis=0)
        bias_ref[...] = jnp.where(picked > 0.0, 0.0, NEG)

        s = jnp.where(causal, _dot_nt(kb, qg_ref[...]), NEG)
        m0 = jnp.max(s, axis=0, keepdims=True)
        p = jnp.exp2(s - m0)
        m_ref[...] = m0
        l_ref[...] = jnp.sum(p, axis=0, keepdims=True)
        acc_ref[...] = _dot(vt, p.astype(BF16))

        def past_block(j):
            s = _dot_nt(kb_ref[j], qg_ref[...])
            bias = bias_ref[pl.ds(j, 1), :]
            m_prev = m_ref[...]
            m_new = jnp.maximum(m_prev, jnp.max(s, axis=0, keepdims=True) + bias)
            alpha = jnp.exp2(m_prev - m_new)
            p = jnp.exp2(s + (bias - m_new))
            l_ref[...] = alpha * l_ref[...] + jnp.sum(p, axis=0, keepdims=True)
            acc_ref[...] = alpha * acc_ref[...] + _dot(vt_ref[j], p.astype(BF16))
            m_ref[...] = m_new

        def two_blocks(t, carry):
            past_block(2 * t)
            past_block(2 * t + 1)
            return carry

        lax.fori_loop(0, qb // 2, two_blocks, 0)

        @pl.when(qb % 2 == 1)
        def _():
            past_block(qb - 1)

        outs.append(acc_ref[...] / l_ref[...])

    row = lax.broadcasted_iota(jnp.int32, (LANES, 1), 0)
    o_t = jnp.where(row < MOBA_HEAD_DIM, outs[0], outs[1])
    for g in range(MOBA_GROUP):
        gs = slice(g * LANES, (g + 1) * LANES)
        o = o_t[:, g * blk:(g + 1) * blk].T
        o_ref[:, gs] = (o * _silu(zc_ref[:, gs])).astype(BF16)


def _moba_prompt(h, hn, n_seq, seq):
    nb = seq // MOBA_BLOCK
    cols = MOBA_GROUP * MOBA_BLOCK
    return pl.pallas_call(
        _moba_prompt_kernel,
        grid=(n_seq, nb),
        in_specs=[pl.BlockSpec((MOBA_BLOCK, W_C), lambda n, b: (n * nb + b, OFF_QC // W_C)),
                  pl.BlockSpec((MOBA_BLOCK, W_CKV), lambda n, b: (n * nb + b, OFF_KC // W_CKV)),
                  pl.BlockSpec((MOBA_BLOCK, W_CKV), lambda n, b: (n * nb + b, OFF_VC // W_CKV)),
                  pl.BlockSpec((MOBA_BLOCK, W_C), lambda n, b: (n * nb + b, OFF_ZC // W_C))],
        out_specs=pl.BlockSpec((MOBA_BLOCK, W_C), lambda n, b: (n * nb + b, 0)),
        out_shape=jax.ShapeDtypeStruct((n_seq * seq, W_C), BF16),
        scratch_shapes=[pltpu.VMEM((nb, MOBA_BLOCK, LANES), BF16),
                        pltpu.VMEM((nb, LANES, MOBA_BLOCK), BF16),
                        pltpu.VMEM((nb, LANES), F32),
                        pltpu.VMEM((cols, LANES), BF16),
                        pltpu.VMEM((nb, cols), F32),
                        pltpu.VMEM((1, cols), F32),
                        pltpu.VMEM((1, cols), F32),
                        pltpu.VMEM((LANES, cols), F32)],
        compiler_params=_params("parallel", "arbitrary"),
        name="moba_prompt",
    )(hn, h, h, hn)


def _moba_decode_one(q, kn, vn, k_refs, v_refs, s_ref, p_ref):
    n_pages = len(k_refs)
    pages_per_block = MOBA_BLOCK // PAGE_SIZE
    n_blocks = n_pages // pages_per_block
    lane = lax.broadcasted_iota(jnp.int32, (1, LANES), 1)

    gate = jnp.full((MOBA_HEADS, LANES), -jnp.inf, F32)
    for b in range(n_blocks):
        tot = jnp.zeros((MOBA_HEADS, PAGE_SIZE), F32)
        for i in range(pages_per_block):
            pg = b * pages_per_block + i
            s = _dot(q, k_refs[pg][...].astype(BF16))
            s_ref[pg] = s
            tot = tot + s
        gate = jnp.where(lane == b, jnp.sum(tot, axis=-1, keepdims=True), gate)
    picked = _topk_mask(gate, lane.astype(F32), MOBA_TOPK, axis=1)

    s_new = jnp.sum(q.astype(F32) * kn, axis=-1, keepdims=True)
    m_lanes = jnp.full((MOBA_HEADS, PAGE_SIZE), NEG, F32)
    for b in range(n_blocks):
        keep = picked[:, b:b + 1] > 0.0
        for i in range(pages_per_block):
            pg = b * pages_per_block + i
            s = jnp.where(keep, s_ref[pg], NEG)
            s_ref[pg] = s
            m_lanes = jnp.maximum(m_lanes, s)
    m = jnp.maximum(jnp.max(m_lanes, axis=-1, keepdims=True), s_new)
    p_new = jnp.exp2(s_new - m)
    l_lanes = jnp.zeros((MOBA_HEADS, PAGE_SIZE), F32)
    for pg in range(n_pages):
        p = jnp.exp2(s_ref[pg] - m)
        l_lanes = l_lanes + p
        p_ref[pg] = p.astype(BF16)
    l = p_new + jnp.sum(l_lanes, axis=-1, keepdims=True)
    acc = p_new * vn
    for pg in range(n_pages):
        acc = acc + _dot_nt(p_ref[pg], v_refs[pg][...].astype(BF16))
    o = acc / l
    return jnp.where(lane < MOBA_HEAD_DIM, o[:MOBA_GROUP], o[MOBA_GROUP:])


def _moba_decode_kernel(pt_ref, q_ref, kn_ref, vn_ref, k_hbm, v_hbm, o_ref, k_buf, v_buf, sems, s_ref, p_ref,
                        *, layer, n_pages):
    slot = _paged_prefetch(pt_ref, layer, n_pages, (k_hbm, v_hbm), (k_buf, v_buf), sems)
    for u in range(DECODE_SEQS):
        b = slot * DECODE_SEQS + u
        o_ref[u] = _moba_decode_one(q_ref[u], kn_ref[u], vn_ref[u],
                                    [k_buf.at[b, i] for i in range(n_pages)],
                                    [v_buf.at[b, i] for i in range(n_pages)], s_ref.at[u], p_ref.at[u])


def _moba_decode(page_table, q, knew, vnew, cache_k_t, cache_v_t, layer):
    b, n_pages = page_table.shape
    u = DECODE_SEQS
    grid_spec = pltpu.PrefetchScalarGridSpec(
        num_scalar_prefetch=1,
        grid=(b // u,),
        in_specs=[pl.BlockSpec((u, MOBA_HEADS, LANES), lambda s, pt: (s, 0, 0)),
                  pl.BlockSpec((u, 1, W_CKV), lambda s, pt: (s, 0, 0)),
                  pl.BlockSpec((u, 1, W_CKV), lambda s, pt: (s, 0, 0)),
                  pl.BlockSpec(memory_space=pl.ANY),
                  pl.BlockSpec(memory_space=pl.ANY)],
        out_specs=pl.BlockSpec((u, MOBA_GROUP, LANES), lambda s, pt: (s, 0, 0)),
        scratch_shapes=[pltpu.VMEM((2 * u, n_pages, W_CKV, PAGE_SIZE), F32),
                        pltpu.VMEM((2 * u, n_pages, W_CKV, PAGE_SIZE), F32),
                        pltpu.SemaphoreType.DMA((2, 2)),
                        pltpu.VMEM((u, n_pages, MOBA_HEADS, PAGE_SIZE), F32),
                        pltpu.VMEM((u, n_pages, MOBA_HEADS, PAGE_SIZE), BF16)])
    return pl.pallas_call(
        functools.partial(_moba_decode_kernel, layer=layer, n_pages=n_pages),
        grid_spec=grid_spec,
        out_shape=jax.ShapeDtypeStruct((b, MOBA_GROUP, LANES), F32),
        compiler_params=_params("arbitrary"),
        name="moba_decode",
    )(page_table, q, knew, vnew, cache_k_t, cache_v_t)


def _sample_out_kernel(ol_ref, oc_ref, zb_ref, zc_ref, wuv_ref, yb_ref, yc_ref):
    for h in range(MLA_HEADS):
        hs = slice(h * V_DIM, (h + 1) * V_DIM)
        y = _dot(ol_ref[h].astype(BF16), wuv_ref[h])
        yb_ref[:, hs] = (y * _silu(zb_ref[:, hs])).astype(BF16)
    yc_ref[...] = (oc_ref[...] * _silu(zc_ref[...])).astype(BF16)


def _sample_out(o_lat, o_c, h, wuv):
    b = h.shape[0]
    return pl.pallas_call(
        _sample_out_kernel,
        grid=(1,),
        in_specs=[pl.BlockSpec((MLA_HEADS, b, KV_LORA), lambda i: (0, 0, 0)),
                  pl.BlockSpec((b, W_C), lambda i: (0, 0)),
                  pl.BlockSpec((b, W_B), lambda i: (0, OFF_ZB // W_B)),
                  pl.BlockSpec((b, W_C), lambda i: (0, OFF_ZC // W_C)),
                  pl.BlockSpec((MLA_HEADS, KV_LORA, V_DIM), lambda i: (0, 0, 0))],
        out_specs=[pl.BlockSpec((b, W_B), lambda i: (0, 0)),
                   pl.BlockSpec((b, W_C), lambda i: (0, 0))],
        out_shape=[jax.ShapeDtypeStruct((b, W_B), BF16),
                   jax.ShapeDtypeStruct((b, W_C), BF16)],
        compiler_params=_params("arbitrary"),
        name="sample_out",
    )(o_lat, o_c, h, h, wuv)


def _merge_kernel(ya_ref, yb_ref, yc_ref, ga_ref, gb_ref, gc_ref, pa_ref, pb_ref, pc_ref, o_ref):
    m = (_sigmoid(ga_ref[...]) * _dot(ya_ref[...], pa_ref[...])
         + _sigmoid(gb_ref[...]) * _dot(yb_ref[...], pb_ref[...])
         + _sigmoid(gc_ref[...]) * _dot(yc_ref[...], pc_ref[...]))
    o_ref[...] = m.astype(BF16)


def _merge(ya, yb, yc, h, pa, pb, pc, tm, tn):
    r = ya.shape[0]
    gate_blk = lambda b: (lambda i, j: (i, (OFF_G + b * D_MODEL) // tn + j))
    row = lambda i, j: (i, 0)
    col = lambda i, j: (0, j)
    return pl.pallas_call(
        _merge_kernel,
        grid=(r // tm, D_MODEL // tn),
        in_specs=[pl.BlockSpec((tm, W_A), row), pl.BlockSpec((tm, W_B), row), pl.BlockSpec((tm, W_C), row),
                  pl.BlockSpec((tm, tn), gate_blk(0)), pl.BlockSpec((tm, tn), gate_blk(1)),
                  pl.BlockSpec((tm, tn), gate_blk(2)),
                  pl.BlockSpec((W_A, tn), col), pl.BlockSpec((W_B, tn), col), pl.BlockSpec((W_C, tn), col)],
        out_specs=pl.BlockSpec((tm, tn), lambda i, j: (i, j)),
        out_shape=jax.ShapeDtypeStruct((r, D_MODEL), BF16),
        compiler_params=_params("parallel", "arbitrary"),
        name="merge",
    )(ya, yb, yc, h, h, h, pa, pb, pc)


def _outproj_kernel(m_ref, w_ref, x_ref, o_ref):
    o_ref[...] = x_ref[...] + _dot(m_ref[...], w_ref[...])


def _outproj(m, w, x, tm, tn):
    r = m.shape[0]
    return pl.pallas_call(
        _outproj_kernel,
        grid=(r // tm, D_MODEL // tn),
        in_specs=[pl.BlockSpec((tm, D_MODEL), lambda i, j: (i, 0)),
                  pl.BlockSpec((D_MODEL, tn), lambda i, j: (0, j)),
                  pl.BlockSpec((tm, tn), lambda i, j: (i, j))],
        out_specs=pl.BlockSpec((tm, tn), lambda i, j: (i, j)),
        out_shape=jax.ShapeDtypeStruct((r, D_MODEL), F32),
        compiler_params=_params("parallel", "arbitrary"),
        name="outproj",
    )(m, w, x)


def _final_norm_kernel(x_ref, g_ref, o_ref):
    x = x_ref[...]
    ms = jnp.mean(x * x, axis=-1, keepdims=True)
    o_ref[...] = x * lax.rsqrt(ms + EPS) * g_ref[...]


def _final_norm(x, g, tm):
    r = x.shape[0]
    return pl.pallas_call(
        _final_norm_kernel,
        grid=(r // tm,),
        in_specs=[pl.BlockSpec((tm, D_MODEL), lambda i: (i, 0)),
                  pl.BlockSpec((1, D_MODEL), lambda i: (0, 0))],
        out_specs=pl.BlockSpec((tm, D_MODEL), lambda i: (i, 0)),
        out_shape=jax.ShapeDtypeStruct((r, D_MODEL), F32),
        compiler_params=_params("parallel"),
        name="final_norm",
    )(x, g)


def _pair_heads(w):
    lead = w.shape[:-1]
    w = w.reshape(lead + (MOBA_KV_HEADS, MOBA_GROUP, MOBA_HEAD_DIM))
    return jnp.swapaxes(w, -3, -2).reshape(lead + (W_C,))


def _swap_halves(w):
    half = ROPE_DIM // 2
    return jnp.concatenate([w[..., half:], w[..., :half]], axis=-1)


def _prep_weights(w_in, w_pool, w_uq, w_uk, w_uv, p_a, p_b, p_c, w_out):
    offs = np.cumsum((0,) + IN_SPLITS)
    w_in_t = jnp.swapaxes(w_in, 1, 2)
    piece = lambda i: w_in_t[:, offs[i]:offs[i + 1], :]
    u_a, z_a, c_q, c_kv, k_pe, z_b, q_c, k_c, v_c, z_c, g = [piece(i) for i in range(len(IN_SPLITS))]
    pair_rows = lambda t: jnp.swapaxes(_pair_heads(jnp.swapaxes(t, 1, 2)), 1, 2)
    swap_rows = lambda t: jnp.swapaxes(_swap_halves(jnp.swapaxes(t, 1, 2)), 1, 2)
    w_in_n = jnp.concatenate([z_b, z_a, pair_rows(q_c), pair_rows(z_c), g], axis=1).astype(BF16)
    w_in_w = jnp.concatenate([u_a, c_q, c_kv, k_c, v_c, k_pe, swap_rows(k_pe)], axis=1).astype(BF16)
    wq = w_uq.reshape(DEPTH, Q_LORA, MLA_HEADS, NOPE_DIM + ROPE_DIM)
    wqn = wq[..., :NOPE_DIM].reshape(DEPTH, Q_LORA, MLA_HEADS * NOPE_DIM).astype(BF16)
    pe = wq[..., NOPE_DIM:]
    wqp = jnp.concatenate([pe, _swap_halves(pe)], axis=-1).reshape(DEPTH, Q_LORA, MLA_HEADS * LANES).astype(BF16)
    wuk = jnp.transpose(w_uk, (0, 2, 3, 1)).astype(BF16)
    wuv = jnp.transpose(w_uv, (0, 2, 1, 3)).astype(BF16)
    p_c_p = jnp.swapaxes(p_c.reshape(DEPTH, MOBA_KV_HEADS, MOBA_GROUP, MOBA_HEAD_DIM, D_MODEL), 1, 2)
    p_c_p = p_c_p.reshape(DEPTH, W_C, D_MODEL).astype(BF16)
    return dict(w_in_n=w_in_n, w_in_w=w_in_w, w_pool=w_pool.astype(BF16), wqn=wqn, wqp=wqp, wuk=wuk, wuv=wuv,
                p_a=p_a.astype(BF16), p_b=p_b.astype(BF16), p_c=p_c_p, w_out=w_out.astype(BF16))


def _rope_table(pos):
    half = ROPE_DIM // 2
    inv = ROPE_THETA ** (-jnp.arange(half, dtype=F32) / half)
    ang = pos.astype(F32)[:, None] * inv[None, :]
    cos, sin = jnp.cos(ang), jnp.sin(ang)
    return jnp.concatenate([cos, cos, -sin, sin], axis=-1)


def _layer_prompt(x, w, l, cs, norm_g, pool_scale, q_norm_g, kv_norm_g, n_seq, seq):
    hn = _inproj(x, norm_g[l][None], w["w_in_n"][l], tm=1024, tn=1024, out_dtype=BF16)
    h = _inproj(x, norm_g[l][None], w["w_in_w"][l], tm=512, tn=N_H32, out_dtype=F32)
    ya = _pool_prompt(h, hn, w["w_pool"][l], pool_scale[l][None], n_seq, seq)
    q, kcat, ckv_t, ckv_new, kpe_new = _mla_prep(h, cs, q_norm_g[l][None], kv_norm_g[l][None],
                                                 w["wqn"][l], w["wqp"][l], w["wuk"][l], tm=256)
    yb = _mla_prompt(q, kcat, ckv_t, hn, w["wuv"][l], n_seq, seq)
    yc = _moba_prompt(h, hn, n_seq, seq)
    m = _merge(ya, yb, yc, hn, w["p_a"][l], w["p_b"][l], w["p_c"][l], tm=1024, tn=512)
    x = _outproj(m, w["w_out"][l], x, tm=1024, tn=512)
    state = (h[:, OFF_UA:OFF_UA + W_A].reshape(n_seq, seq, W_A)[:, seq - POOL_HIST:],
             ckv_new.reshape(n_seq, seq, KV_LORA),
             kpe_new.reshape(n_seq, seq, ROPE_DIM),
             h[:, OFF_KC:OFF_KC + W_CKV].reshape(n_seq, seq, MOBA_KV_HEADS, MOBA_HEAD_DIM),
             h[:, OFF_VC:OFF_VC + W_CKV].reshape(n_seq, seq, MOBA_KV_HEADS, MOBA_HEAD_DIM))
    return x, state


def _layer_sample(x, w, l, cs, norm_g, pool_scale, q_norm_g, kv_norm_g,
                  hist2d, cache_ckv, cache_kpe_t, cache_k_t, cache_v_t, page_table):
    b = x.shape[0]
    hn = _inproj(x, norm_g[l][None], w["w_in_n"][l], tm=b, tn=1024, out_dtype=BF16)
    h = _inproj(x, norm_g[l][None], w["w_in_w"][l], tm=b, tn=N_H32, out_dtype=F32)
    ya, pool_new = _pool_sample(hist2d, h, hn, w["w_pool"][l], pool_scale[l][None])
    q, kcat, _, ckv_new, kpe_new = _mla_prep(h, cs, q_norm_g[l][None], kv_norm_g[l][None],
                                             w["wqn"][l], w["wqp"][l], w["wuk"][l], tm=b)
    o_lat = _mla_decode(page_table, jnp.swapaxes(q, 0, 1), kcat[:, None, :], cache_ckv, cache_kpe_t, l)
    qc = hn[:, OFF_QC:OFF_QC + W_C].astype(F32).reshape(b, 1, MOBA_GROUP, LANES) * MOBA_SCALE
    half = (jnp.arange(LANES)[None, None, None, :] // MOBA_HEAD_DIM
            == jnp.arange(MOBA_KV_HEADS)[None, :, None, None])
    qd = jnp.where(half, qc, 0.0).reshape(b, MOBA_HEADS, LANES).astype(BF16)
    k_new = h[:, OFF_KC:OFF_KC + W_CKV]
    v_new = h[:, OFF_VC:OFF_VC + W_CKV]
    o_c = _moba_decode(page_table, qd, k_new[:, None, :], v_new[:, None, :], cache_k_t, cache_v_t, l)
    yb, yc = _sample_out(jnp.swapaxes(o_lat, 0, 1), o_c.reshape(b, W_C), hn, w["wuv"][l])
    m = _merge(ya, yb, yc, hn, w["p_a"][l], w["p_b"][l], w["p_c"][l], tm=b, tn=512)
    x = _outproj(m, w["w_out"][l], x, tm=b, tn=512)
    state = (pool_new.reshape(b, POOL_HIST, W_A),
             ckv_new.reshape(b, 1, KV_LORA),
             kpe_new.reshape(b, 1, ROPE_DIM),
             k_new.reshape(b, 1, MOBA_KV_HEADS, MOBA_HEAD_DIM),
             v_new.reshape(b, 1, MOBA_KV_HEADS, MOBA_HEAD_DIM))
    return x, state


@jax.jit
def _forward(x_prompt, x_sample, state_pool, cache_ckv, cache_kpe, cache_k, cache_v, page_table,
             norm_g, w_in, w_pool, pool_scale, q_norm_g, kv_norm_g, w_uq, w_uk, w_uv,
             p_a, p_b, p_c, w_out, final_g):
    n_seq, seq, _ = x_prompt.shape
    b = x_sample.shape[0]
    n_pool = cache_k.shape[1]
    past_len = page_table.shape[1] * PAGE_SIZE
    w = _prep_weights(w_in, w_pool, w_uq, w_uk, w_uv, p_a, p_b, p_c, w_out)
    cs_p = jnp.tile(_rope_table(jnp.arange(seq)), (n_seq, 1))
    cs_s = jnp.tile(_rope_table(jnp.full((1,), past_len)), (b, 1))
    cache_kpe_t = jnp.swapaxes(cache_kpe, 2, 3)
    cache_k_t = jnp.transpose(cache_k, (0, 1, 3, 4, 2)).reshape(DEPTH, n_pool, W_CKV, PAGE_SIZE)
    cache_v_t = jnp.transpose(cache_v, (0, 1, 3, 4, 2)).reshape(DEPTH, n_pool, W_CKV, PAGE_SIZE)
    hist2d = state_pool.reshape(DEPTH, b, POOL_HIST * W_A)

    xp = x_prompt.reshape(n_seq * seq, D_MODEL)
    xs = x_sample.reshape(b, D_MODEL)
    new_p, new_s = [], []
    for l in range(DEPTH):
        xs, st_s = _layer_sample(xs, w, l, cs_s, norm_g, pool_scale, q_norm_g, kv_norm_g,
                                 hist2d[l], cache_ckv, cache_kpe_t, cache_k_t, cache_v_t, page_table)
        xp, st_p = _layer_prompt(xp, w, l, cs_p, norm_g, pool_scale, q_norm_g, kv_norm_g, n_seq, seq)
        new_p.append(st_p)
        new_s.append(st_s)
    y_prompt = _final_norm(xp, final_g[None], tm=1024).reshape(n_seq, seq, D_MODEL)
    y_sample = _final_norm(xs, final_g[None], tm=b).reshape(b, 1, D_MODEL)
    stack = lambda sts, i: jnp.stack([st[i] for st in sts])
    return ((y_prompt, y_sample)
            + tuple(stack(new_p, i) for i in range(5))
            + tuple(stack(new_s, i) for i in range(5)))


def kernel(x_prompt, x_sample, state_pool, cache_ckv, cache_kpe, cache_k, cache_v, page_table,
           norm_g, w_in, w_pool, pool_scale, q_norm_g, kv_norm_g, w_uq, w_uk, w_uv,
           p_a, p_b, p_c, w_out, final_g):
    return _forward(x_prompt, x_sample, state_pool, cache_ckv, cache_kpe, cache_k, cache_v, page_table,
                    norm_g, w_in, w_pool, pool_scale, q_norm_g, kv_norm_g, w_uq, w_uk, w_uv,
                    p_a, p_b, p_c, w_out, final_g)
```
